```python
import math
import jax, jax.numpy as jnp
from jax import lax
import numpy as np

D_MODEL = 1024
BATCH = 8
SEQ = 4096
DEPTH = 2

GRID_W = 64
CTX_LEN = 256
BLOCK = 128
HEAD_DIM = 64
ROPE_THETA = 10000.0
ROPE_AXIS_PAIRS = HEAD_DIM // 4
EPS = 1e-6
NEG = -1e30
MIX_HALF = D_MODEL // 2

A_HEADS = MIX_HALF // HEAD_DIM
A_KV_HEADS = A_HEADS // 4
A_WINDOW = 128
B_VDIM = 2 * HEAD_DIM
B_HEADS = MIX_HALF // B_VDIM
C_WIDTH = MIX_HALF
C_GROUPS = 4
C_GROUP_DIM = C_WIDTH // C_GROUPS
C_CHUNK = 128
D_NOPE = 64
D_ROPE = 64
D_VDIM = 128
D_HEADS = MIX_HALF // D_VDIM
D_Q_RANK = D_MODEL // 4
D_KV_RANK = D_MODEL // 4
FFN_HIDDEN = ((-(-8 * D_MODEL // 3)) + 255) // 256 * 256

A_Q_DIM = A_HEADS * HEAD_DIM
A_KV_DIM = A_KV_HEADS * HEAD_DIM
B_QK_DIM = B_HEADS * 2 * HEAD_DIM
B_V_TOTAL = B_HEADS * B_VDIM
EVEN_Q = A_Q_DIM + B_QK_DIM
EVEN_KV = 2 * A_KV_DIM + B_QK_DIM + B_V_TOTAL
EVEN_IN = EVEN_Q + EVEN_KV
ODD_C = 2 * C_WIDTH
ODD_KV = D_KV_RANK + D_ROPE
ODD_IN = ODD_C + D_Q_RANK + ODD_KV
UQ_DIM = D_HEADS * (D_NOPE + D_ROPE)
UKV_DIM = D_HEADS * (D_NOPE + D_VDIM)

kernel_name = "hybrid_dit_prefix_ctx_block"


def rms_norm(x, g):
    xf = x.astype(jnp.float32)
    y = xf * lax.rsqrt(jnp.mean(xf * xf, axis=-1, keepdims=True) + EPS)
    return (y * g.astype(jnp.float32)).astype(x.dtype)


def axial_rope(n_tokens):
    rows = n_tokens // GRID_W
    row = jnp.repeat(jnp.arange(rows, dtype=jnp.int32), GRID_W).astype(jnp.float32)
    col = jnp.tile(jnp.arange(GRID_W, dtype=jnp.int32), rows).astype(jnp.float32)
    inv = ROPE_THETA ** (-jnp.arange(ROPE_AXIS_PAIRS, dtype=jnp.float32) / ROPE_AXIS_PAIRS)
    ang = jnp.concatenate([row[:, None] * inv, col[:, None] * inv], axis=-1)
    return jnp.cos(ang)[:, None, :], jnp.sin(ang)[:, None, :]


def apply_rope(x, cos, sin):
    x1, x2 = jnp.split(x, 2, axis=-1)
    return jnp.concatenate([x1 * cos - x2 * sin, x2 * cos + x1 * sin], axis=-1).astype(x.dtype)


def adaln(cvec, w, b, n):
    m = jax.nn.silu(cvec) @ w[:, :n * D_MODEL] + b[:n * D_MODEL]
    return jnp.split(m[..., None, :], n, axis=-1)


def modulate(h, shift, scale):
    return h * (1.0 + scale) + shift


def swiglu(h, w_in, w_out):
    g, u = jnp.split(h @ w_in, 2, axis=-1)
    return (jax.nn.silu(g) * u) @ w_out


def sweep_query_blocks(fn, qs):
    b_, s = qs[0].shape[:2]
    nb = s // BLOCK
    blocks = tuple(q.reshape((b_, nb, BLOCK) + q.shape[2:]).swapaxes(0, 1) for q in qs)
    out = lax.map(lambda t: fn(*t), blocks)
    return out.swapaxes(0, 1).reshape((b_, s) + out.shape[3:])


def window_gqa_sink_latent(q, k, v, kc, vc, sink):
    b_, s, h, d = q.shape
    nb = s // BLOCK
    grp = h // A_KV_HEADS
    scale = d ** -0.5
    qb = q.reshape(b_, nb, BLOCK, A_KV_HEADS, grp, d).swapaxes(0, 1)
    pad = ((0, 0), (BLOCK, BLOCK), (0, 0), (0, 0))
    kp = jnp.pad(k, pad)
    vp = jnp.pad(v, pad)
    offs = jnp.arange(3 * BLOCK) - BLOCK
    qi = jnp.arange(BLOCK)
    band = jnp.abs(qi[:, None] - offs[None, :]) <= A_WINDOW
    sink_l = sink.astype(jnp.float32).reshape(A_KV_HEADS, grp, 1, 1)

    def one_block(args):
        qn, n = args
        start = n * BLOCK
        kn = lax.dynamic_slice_in_dim(kp, start, 3 * BLOCK, axis=1)
        vn = lax.dynamic_slice_in_dim(vp, start, 3 * BLOCK, axis=1)
        kpos = start + offs
        mask = band & ((kpos >= 0) & (kpos < s))[None, :]
        s_loc = jnp.einsum('bqkgd,bmkd->bkgqm', qn, kn).astype(jnp.float32) * scale
        s_loc = jnp.where(mask, s_loc, NEG)
        s_ctx = jnp.einsum('bqkgd,blkd->bkgql', qn, kc).astype(jnp.float32) * scale
        snk = jnp.broadcast_to(sink_l, s_loc.shape[:-1] + (1,))
        p = jax.nn.softmax(jnp.concatenate([s_loc, s_ctx, snk], axis=-1), axis=-1).astype(v.dtype)
        return (jnp.einsum('bkgqm,bmkd->bqkgd', p[..., :3 * BLOCK], vn)
                + jnp.einsum('bkgql,blkd->bqkgd', p[..., 3 * BLOCK:-1], vc))

    out = lax.map(one_block, (qb, jnp.arange(nb)))
    return out.swapaxes(0, 1).reshape(b_, s, h * d)


def ctx_gqa_sink(qc, kc, vc, sink):
    b_, l, h, d = qc.shape
    grp = h // A_KV_HEADS
    qg = qc.reshape(b_, l, A_KV_HEADS, grp, d)
    sc = jnp.einsum('bqkgd,blkd->bkgql', qg, kc).astype(jnp.float32) * d ** -0.5
    snk = jnp.broadcast_to(sink.astype(jnp.float32).reshape(A_KV_HEADS, grp, 1, 1), sc.shape[:-1] + (1,))
    p = jax.nn.softmax(jnp.concatenate([sc, snk], axis=-1), axis=-1)[..., :-1].astype(vc.dtype)
    return jnp.einsum('bkgql,blkd->bqkgd', p, vc).reshape(b_, l, h * d)


def diff_core(q1, q2, k1, k2, v, lam, scale):
    p1 = jax.nn.softmax(jnp.einsum('bqhd,bkhd->bhqk', q1, k1).astype(jnp.float32) * scale, axis=-1)
    p2 = jax.nn.softmax(jnp.einsum('bqhd,bkhd->bhqk', q2, k2).astype(jnp.float32) * scale, axis=-1)
    w = (p1 - lam * p2).astype(v.dtype)
    return jnp.einsum('bhqk,bkhe->bqhe', w, v)


def diff_head_norm(o, g, lam_init):
    b_, n = o.shape[:2]
    return (rms_norm(o, g) * (1.0 - lam_init)).reshape(b_, n, B_V_TOTAL)


def diff_lambda_init(layer_idx):
    return 0.8 - 0.6 * math.exp(-0.3 * layer_idx)


def even_queries(t, qn_a, qn_b):
    b_, n = t.shape[:2]
    qa = rms_norm(t[..., :A_Q_DIM].reshape(b_, n, A_HEADS, HEAD_DIM), qn_a)
    qb = rms_norm(t[..., A_Q_DIM:].reshape(b_, n, B_HEADS, 2, HEAD_DIM), qn_b)
    return qa, qb[..., 0, :], qb[..., 1, :]


def even_keys_values(t, kn_a, kn_b):
    b_, n = t.shape[:2]
    o1 = A_KV_DIM
    o2 = 2 * A_KV_DIM
    o3 = o2 + B_QK_DIM
    ka = rms_norm(t[..., :o1].reshape(b_, n, A_KV_HEADS, HEAD_DIM), kn_a)
    va = t[..., o1:o2].reshape(b_, n, A_KV_HEADS, HEAD_DIM)
    kb = rms_norm(t[..., o2:o3].reshape(b_, n, B_HEADS, 2, HEAD_DIM), kn_b)
    vb = t[..., o3:].reshape(b_, n, B_HEADS, B_VDIM)
    return ka, va, kb[..., 0, :], kb[..., 1, :], vb


def even_mixers(h, hc, cos, sin, w_in, qn_a, kn_a, sink, qn_b, kn_b,
                lq1, lk1, lq2, lk2, subln, lam_init, ctx_out):
    pl = h @ w_in
    pc = hc @ (w_in if ctx_out else w_in[:, EVEN_Q:])
    qa, q1, q2 = even_queries(pl[..., :EVEN_Q], qn_a, qn_b)
    ka, va, k1, k2, vb = even_keys_values(pl[..., EVEN_Q:], kn_a, kn_b)
    kac, vac, k1c, k2c, vbc = even_keys_values(pc[..., -EVEN_KV:], kn_a, kn_b)
    qa, q1, q2 = apply_rope(qa, cos, sin), apply_rope(q1, cos, sin), apply_rope(q2, cos, sin)
    ka, k1, k2 = apply_rope(ka, cos, sin), apply_rope(k1, cos, sin), apply_rope(k2, cos, sin)
    f32 = jnp.float32
    lam = (jnp.exp(jnp.sum(lq1.astype(f32) * lk1.astype(f32)))
           - jnp.exp(jnp.sum(lq2.astype(f32) * lk2.astype(f32))) + lam_init)
    scale_b = HEAD_DIM ** -0.5

    out_a = window_gqa_sink_latent(qa, ka, va, kac, vac, sink)
    k1_all = jnp.concatenate([k1, k1c], axis=1)
    k2_all = jnp.concatenate([k2, k2c], axis=1)
    v_all = jnp.concatenate([vb, vbc], axis=1)
    ob = sweep_query_blocks(lambda a, b: diff_core(a, b, k1_all, k2_all, v_all, lam, scale_b), (q1, q2))
    out_b = diff_head_norm(ob, subln, lam_init)
    mix = jnp.concatenate([out_a, out_b], axis=-1)
    if not ctx_out:
        return mix, None
    qac, q1c, q2c = even_queries(pc[..., :EVEN_Q], qn_a, qn_b)
    out_ac = ctx_gqa_sink(qac, kac, vac, sink)
    out_bc = diff_head_norm(diff_core(q1c, q2c, k1c, k2c, vbc, lam, scale_b), subln, lam_init)
    return mix, jnp.concatenate([out_ac, out_bc], axis=-1)


def spatial_gating(uv, v_norm_g, w_s, b_s):
    b_, n, _ = uv.shape
    u, v = jnp.split(jax.nn.gelu(uv, approximate=False), 2, axis=-1)
    v = rms_norm(v, v_norm_g).reshape(b_, n // C_CHUNK, C_CHUNK, C_GROUPS, C_GROUP_DIM)
    vs = jnp.einsum('gpq,bnqgc->bnpgc', w_s, v) + b_s.T[:, :, None]
    return u * vs.reshape(b_, n, C_WIDTH)


def mla_queries(cq, qa_norm, w_uq, qn_nope, qn_rope):
    b_, n = cq.shape[:2]
    q = (rms_norm(cq, qa_norm) @ w_uq).reshape(b_, n, D_HEADS, D_NOPE + D_ROPE)
    return rms_norm(q[..., :D_NOPE], qn_nope), rms_norm(q[..., D_NOPE:], qn_rope)


def mla_keys_values(t, kva_norm, w_ukv, kn_nope, kn_rope):
    b_, n = t.shape[:2]
    kv = (rms_norm(t[..., :D_KV_RANK], kva_norm) @ w_ukv).reshape(b_, n, D_HEADS, D_NOPE + D_VDIM)
    k_pe = rms_norm(t[..., D_KV_RANK:], kn_rope)
    return rms_norm(kv[..., :D_NOPE], kn_nope), k_pe, kv[..., D_NOPE:]


def mla_core(qn, qp, kn, kp, v, scale):
    sc = (jnp.einsum('bqhd,bkhd->bhqk', qn, kn) + jnp.einsum('bqhr,bkr->bhqk', qp, kp)).astype(jnp.float32) * scale
    p = jax.nn.softmax(sc, axis=-1).astype(v.dtype)
    return jnp.einsum('bhqk,bkhe->bqhe', p, v)


def odd_mixers(h, hc, cos, sin, w_in, c_vnorm, c_ws, c_bs, qa_norm, kva_norm, w_uq, w_ukv,
               qn_nope, kn_nope, qn_rope, kn_rope, ctx_out):
    b_, s = h.shape[:2]
    pl = h @ w_in
    pc = hc @ (w_in if ctx_out else w_in[:, ODD_C + D_Q_RANK:])
    out_c = spatial_gating(pl[..., :ODD_C], c_vnorm, c_ws, c_bs)
    qn, qp = mla_queries(pl[..., ODD_C:ODD_C + D_Q_RANK], qa_norm, w_uq, qn_nope, qn_rope)
    kn, kp, v = mla_keys_values(pl[..., ODD_C + D_Q_RANK:], kva_norm, w_ukv, kn_nope, kn_rope)
    knc, kpc, vc = mla_keys_values(pc[..., -ODD_KV:], kva_norm, w_ukv, kn_nope, kn_rope)
    qp = apply_rope(qp, cos, sin)
    kp = apply_rope(kp[:, :, None, :], cos, sin)[:, :, 0, :]
    kn_all = jnp.concatenate([kn, knc], axis=1)
    kp_all = jnp.concatenate([kp, kpc], axis=1)
    v_all = jnp.concatenate([v, vc], axis=1)
    scale = (D_NOPE + D_ROPE) ** -0.5
    od = sweep_query_blocks(lambda a, b: mla_core(a, b, kn_all, kp_all, v_all, scale), (qn, qp))
    mix = jnp.concatenate([out_c, od.reshape(b_, s, D_HEADS * D_VDIM)], axis=-1)
    if not ctx_out:
        return mix, None
    l = hc.shape[1]
    out_cc = spatial_gating(pc[..., :ODD_C], c_vnorm, c_ws, c_bs)
    qnc, qpc = mla_queries(pc[..., ODD_C:ODD_C + D_Q_RANK], qa_norm, w_uq, qn_nope, qn_rope)
    out_dc = mla_core(qnc, qpc, knc, kpc, vc, scale).reshape(b_, l, D_HEADS * D_VDIM)
    return mix, jnp.concatenate([out_cc, out_dc], axis=-1)


def setup_inputs(seed: int = 0) -> dict:
    key = jax.random.key(seed)
    ks = iter(jax.random.split(key, 48))
    f32 = jnp.float32
    D = D_MODEL
    ne = (DEPTH + 1) // 2
    no = DEPTH // 2

    def nrm(shape, scale):
        return scale * jax.random.normal(next(ks), shape, f32)

    def gain(shape):
        return 1.0 + 0.02 * jax.random.normal(next(ks), shape, f32)

    return {
        "x": nrm((BATCH, SEQ, D), 1.0),
        "c": nrm((BATCH, D), 1.0),
        "ctx": nrm((BATCH, CTX_LEN, D), 1.0),
        "c_ctx": nrm((D,), 1.0),
        "norm1_g": gain((DEPTH, D)),
        "norm2_g": gain((DEPTH, D)),
        "ada_w": nrm((DEPTH, D, 6 * D), 0.5 * D ** -0.5),
        "ada_b": nrm((DEPTH, 6 * D), 0.01),
        "mix_w_out": nrm((DEPTH, D, D), D ** -0.5),
        "ffn_w_in": nrm((DEPTH, D, 2 * FFN_HIDDEN), D ** -0.5),
        "ffn_w_out": nrm((DEPTH, FFN_HIDDEN, D), FFN_HIDDEN ** -0.5),
        "ev_w_in": nrm((ne, D, EVEN_IN), D ** -0.5),
        "ev_qnorm_a": gain((ne, HEAD_DIM)),
        "ev_knorm_a": gain((ne, HEAD_DIM)),
        "ev_sink": nrm((ne, A_HEADS), 0.5),
        "ev_qnorm_b": gain((ne, HEAD_DIM)),
        "ev_knorm_b": gain((ne, HEAD_DIM)),
        "ev_lam_q1": nrm((ne, HEAD_DIM), 0.1),
        "ev_lam_k1": nrm((ne, HEAD_DIM), 0.1),
        "ev_lam_q2": nrm((ne, HEAD_DIM), 0.1),
        "ev_lam_k2": nrm((ne, HEAD_DIM), 0.1),
        "ev_subln": gain((ne, B_VDIM)),
        "od_w_in": nrm((no, D, ODD_IN), D ** -0.5),
        "od_c_vnorm": gain((no, C_WIDTH)),
        "od_c_ws": nrm((no, C_GROUPS, C_CHUNK, C_CHUNK), C_CHUNK ** -0.5),
        "od_c_bs": gain((no, C_GROUPS, C_CHUNK)),
        "od_qa_norm": gain((no, D_Q_RANK)),
        "od_kva_norm": gain((no, D_KV_RANK)),
        "od_w_uq": nrm((no, D_Q_RANK, UQ_DIM), D_Q_RANK ** -0.5),
        "od_w_ukv": nrm((no, D_KV_RANK, UKV_DIM), D_KV_RANK ** -0.5),
        "od_qnorm_nope": gain((no, D_NOPE)),
        "od_knorm_nope": gain((no, D_NOPE)),
        "od_qnorm_rope": gain((no, D_ROPE)),
        "od_knorm_rope": gain((no, D_ROPE)),
    }


def reference(x, c, ctx, c_ctx, norm1_g, norm2_g, ada_w, ada_b, mix_w_out, ffn_w_in, ffn_w_out,
              ev_w_in, ev_qnorm_a, ev_knorm_a, ev_sink, ev_qnorm_b, ev_knorm_b,
              ev_lam_q1, ev_lam_k1, ev_lam_q2, ev_lam_k2, ev_subln,
              od_w_in, od_c_vnorm, od_c_ws, od_c_bs, od_qa_norm, od_kva_norm, od_w_uq, od_w_ukv,
              od_qnorm_nope, od_knorm_nope, od_qnorm_rope, od_knorm_rope):
    n_lat = x.shape[1]
    cos, sin = axial_rope(n_lat)
    xc = ctx
    for i in range(DEPTH):
        last = i == DEPTH - 1
        j = i // 2
        ml = adaln(c, ada_w[i], ada_b[i], 6)
        mc = adaln(c_ctx, ada_w[i], ada_b[i], 2 if last else 6)
        h = modulate(rms_norm(x, norm1_g[i]), ml[0], ml[1])
        hc = modulate(rms_norm(xc, norm1_g[i]), mc[0], mc[1])
        if i % 2 == 0:
            mix, mix_c = even_mixers(h, hc, cos, sin, ev_w_in[j], ev_qnorm_a[j], ev_knorm_a[j], ev_sink[j],
                                     ev_qnorm_b[j], ev_knorm_b[j], ev_lam_q1[j], ev_lam_k1[j],
                                     ev_lam_q2[j], ev_lam_k2[j], ev_subln[j], diff_lambda_init(i), not last)
        else:
            mix, mix_c = odd_mixers(h, hc, cos, sin, od_w_in[j], od_c_vnorm[j], od_c_ws[j], od_c_bs[j],
                                    od_qa_norm[j], od_kva_norm[j], od_w_uq[j], od_w_ukv[j],
                                    od_qnorm_nope[j], od_knorm_nope[j], od_qnorm_rope[j], od_knorm_rope[j],
                                    not last)
        x = x + ml[2] * (mix @ mix_w_out[i])
        h2 = modulate(rms_norm(x, norm2_g[i]), ml[3], ml[4])
        x = x + ml[5] * swiglu(h2, ffn_w_in[i], ffn_w_out[i])
        if not last:
            xc = xc + mc[2] * (mix_c @ mix_w_out[i])
            hc2 = modulate(rms_norm(xc, norm2_g[i]), mc[3], mc[4])
            xc = xc + mc[5] * swiglu(hc2, ffn_w_in[i], ffn_w_out[i])
    return x
```

```python
import functools
import math

import numpy as np
import jax
import jax.numpy as jnp
from jax import lax
from jax.experimental import pallas as pl
from jax.experimental.pallas import tpu as pltpu

F32 = jnp.float32
BF16 = jnp.bfloat16

D_MODEL = 1024
DEPTH = 2
GRID_W = 64
HEAD_DIM = 64
HALF = HEAD_DIM // 2
ROPE_THETA = 10000.0
ROPE_AXIS_PAIRS = HEAD_DIM // 4
EPS = 1e-6
NEG = -1e30
MIX_HALF = D_MODEL // 2
A_HEADS = 8
A_KV_HEADS = 2
A_WINDOW = 128
B_HEADS = 4
B_VDIM = 128
C_WIDTH = MIX_HALF
C_GROUPS = 4
C_CHUNK = 128
D_NOPE = 64
D_ROPE = 64
D_VDIM = 128
D_HEADS = 4
D_Q_RANK = 256
D_KV_RANK = 256
FFN_HIDDEN = 2816
EVEN_IN = 2304
ODD_C = 2 * C_WIDTH

LANES = 128
MXU_DIM = 256
VMEM_LIMIT = 56 * 1024 * 1024

EV_QA, EV_QB, EV_KA, EV_KB, EV_VA, EV_VB = 0, 4, 8, 9, 13, 14
EV_NORM_TILES = 13
OD_C, OD_Q, OD_K, OD_V = 0, 4, 8, 12
OD_OUT = 2048
OD_IN_PAD = 1664


def _dot(a, b):
    return jnp.dot(a, b, preferred_element_type=F32)


def _dot_nt(a, b):
    return lax.dot_general(a, b, (((1,), (1,)), ((), ())), preferred_element_type=F32)


def _lane_is_a(shape):
    lane = lax.broadcasted_iota(jnp.int32, shape, len(shape) - 1)
    return (lane // HALF) % 2 == 0


def _modulated_norm(x, g, shift, scale):
    ms = jnp.mean(x * x, axis=-1, keepdims=True)
    return (x * lax.rsqrt(ms + EPS) * g) * (1.0 + scale) + shift


def _norm_rope_tiles(t, m_ref, gains_ref, n_tiles, cos, sin):
    out = []
    j = 0
    while j < n_tiles:
        width = MXU_DIM if j + 1 < n_tiles else LANES
        tp = t[:, j * LANES:j * LANES + width]
        msq = _dot((tp * tp).astype(BF16), m_ref[:width, :width])
        for q in range(width // LANES):
            tt = tp[:, q * LANES:(q + 1) * LANES]
            r = lax.rsqrt(msq[:, q * LANES:(q + 1) * LANES] + EPS)
            tn = tt * r * gains_ref[j + q:j + q + 1, :]
            if cos is not None:
                tn = tn * cos + pltpu.roll(tn, 2 * HALF, 1) * sin
            out.append(tn.astype(BF16))
        j += width // LANES
    return out


def _adaln_kernel(cv_ref, w_ref, b_ref, o_ref):
    a = cv_ref[...]
    a = a * jax.nn.sigmoid(a)
    w = w_ref[...]
    a_hi = a.astype(BF16)
    a_lo = (a - a_hi.astype(F32)).astype(BF16)
    w_hi = w.astype(BF16)
    w_lo = (w - w_hi.astype(F32)).astype(BF16)
    o_ref[...] = _dot(a_hi, w_hi) + _dot(a_hi, w_lo) + _dot(a_lo, w_hi) + b_ref[...]


def _adaln(cv, ada_w, ada_b):
    tn = 768
    n6 = 6 * D_MODEL
    rows = cv.shape[0]
    return pl.pallas_call(
        _adaln_kernel,
        grid=(DEPTH, n6 // tn),
        in_specs=[
            pl.BlockSpec((rows, D_MODEL), lambda l, j: (0, 0)),
            pl.BlockSpec((None, D_MODEL, tn), lambda l, j: (l, 0, j)),
            pl.BlockSpec((None, 1, tn), lambda l, j: (l, 0, j)),
        ],
        out_specs=pl.BlockSpec((None, rows, tn), lambda l, j: (l, 0, j)),
        out_shape=jax.ShapeDtypeStruct((DEPTH, rows, n6), F32),
        compiler_params=pltpu.CompilerParams(
            dimension_semantics=("arbitrary", "arbitrary"), vmem_limit_bytes=VMEM_LIMIT),
        name="adaln",
    )(cv, ada_w, ada_b.reshape(DEPTH, 1, n6))


def _proj_even_kernel(*refs, rope):
    if rope:
        x_ref, mod_ref, g_ref, w_ref, m_ref, gains_ref, cos_ref, sin_ref, o_ref = refs
        cos, sin = cos_ref[...], sin_ref[...]
    else:
        x_ref, mod_ref, g_ref, w_ref, m_ref, gains_ref, o_ref = refs
        cos = sin = None
    h = _modulated_norm(x_ref[...], g_ref[...], mod_ref[:, 0:D_MODEL], mod_ref[:, D_MODEL:2 * D_MODEL])
    t = _dot(h.astype(BF16), w_ref[...])
    tiles = _norm_rope_tiles(t, m_ref, gains_ref, EV_NORM_TILES, cos, sin)
    for j, tile in enumerate(tiles):
        o_ref[:, j * LANES:(j + 1) * LANES] = tile
    o_ref[:, EV_NORM_TILES * LANES:] = t[:, EV_NORM_TILES * LANES:].astype(BF16)


def _const_spec(shape):
    nd = len(shape)
    return pl.BlockSpec(shape, lambda *_: (0,) * nd, pipeline_mode=pl.Buffered(1))


def _proj_even(xs, mods, mod_row, g, w, m256, gains, cos, sin, tm):
    bsz, n, _ = xs.shape
    rope = cos is not None
    in_specs = [
        pl.BlockSpec((None, tm, D_MODEL), lambda b, i: (b, i, 0)),
        pl.BlockSpec((None, 1, 6 * D_MODEL), lambda b, i: (mod_row(b), 0, 0)),
        _const_spec((1, D_MODEL)),
        _const_spec((D_MODEL, EVEN_IN)),
        _const_spec((MXU_DIM, MXU_DIM)),
        _const_spec(gains.shape),
    ]
    args = [xs, mods, g, w, m256, gains]
    if rope:
        in_specs += [pl.BlockSpec((tm, LANES), lambda b, i: (i, 0))] * 2
        args += [cos, sin]
    return pl.pallas_call(
        functools.partial(_proj_even_kernel, rope=rope),
        grid=(bsz, n // tm),
        in_specs=in_specs,
        out_specs=pl.BlockSpec((None, tm, EVEN_IN), lambda b, i: (b, i, 0)),
        out_shape=jax.ShapeDtypeStruct((bsz, n, EVEN_IN), BF16),
        compiler_params=pltpu.CompilerParams(
            dimension_semantics=("parallel", "parallel"), vmem_limit_bytes=VMEM_LIMIT),
        name="proj_even_lat" if rope else "proj_even_ctx",
    )(*args)


def _attn_a_kernel(*refs, local, tq, seq):
    if local:
        sink_ref, q_ref, kp_ref, kc_ref, kn_ref, vp_ref, vc_ref, vn_ref, kx_ref, vx_ref, o_ref = refs
        k_loc = jnp.concatenate([kp_ref[...], kc_ref[...], kn_ref[...]], axis=0)
        v_loc = jnp.concatenate([vp_ref[...], vc_ref[...], vn_ref[...]], axis=0)
        tk = tq + 2 * A_WINDOW
        rows = (A_HEADS // A_KV_HEADS) * tq
        qi = lax.broadcasted_iota(jnp.int32, (rows, tk), 0) & (tq - 1)
        off = lax.broadcasted_iota(jnp.int32, (rows, tk), 1) - A_WINDOW
        kpos = pl.program_id(1) * tq + off
        band = (jnp.abs(qi - off) <= A_WINDOW) & (kpos >= 0) & (kpos < seq)
    else:
        sink_ref, q_ref, kx_ref, vx_ref, o_ref = refs
    grp = A_HEADS // A_KV_HEADS
    q = q_ref[...]
    qs = jnp.concatenate([q[:, j * LANES:(j + 1) * LANES] for j in range(grp)], axis=0)
    kx = kx_ref[...]
    vx = vx_ref[...]
    is_a = _lane_is_a((1, LANES))
    outs = []
    for kvh in range(A_KV_HEADS):
        sel = is_a if kvh == 0 else jnp.logical_not(is_a)
        s_ctx = _dot_nt(qs, jnp.where(sel, kx, jnp.zeros_like(kx)))
        sink = jnp.concatenate(
            [jnp.full((tq, 1), sink_ref[kvh * grp + j], F32) for j in range(grp)], axis=0)
        m = jnp.maximum(jnp.max(s_ctx, axis=-1, keepdims=True), sink)
        if local:
            s_loc = _dot_nt(qs, jnp.where(sel, k_loc, jnp.zeros_like(k_loc)))
            s_loc = jnp.where(band, s_loc, NEG)
            m = jnp.maximum(m, jnp.max(s_loc, axis=-1, keepdims=True))
        e_ctx = jnp.exp(s_ctx - m)
        l = jnp.sum(e_ctx, axis=-1, keepdims=True) + jnp.exp(sink - m)
        o = _dot(e_ctx.astype(BF16), vx)
        if local:
            e_loc = jnp.exp(s_loc - m)
            l = l + jnp.sum(e_loc, axis=-1, keepdims=True)
            o = o + _dot(e_loc.astype(BF16), v_loc)
        outs.append(o / l)
    lane = lax.broadcasted_iota(jnp.int32, (1, LANES), 1)
    res = jnp.where(lane < HEAD_DIM, outs[0], outs[1])
    for j in range(grp):
        o_ref[:, j * LANES:(j + 1) * LANES] = res[j * tq:(j + 1) * tq].astype(BF16)


def _attn_a(qsrc, pc, sink, local, tq):
    bsz, n, _ = qsrc.shape
    lctx = pc.shape[1]
    nq = n // tq
    r = tq // A_WINDOW
    nblk = n // A_WINDOW
    qw = MIX_HALF
    in_specs = [pl.BlockSpec(memory_space=pltpu.SMEM),
                pl.BlockSpec((None, tq, qw), lambda b, i: (b, i, 0))]
    args = [sink, qsrc]
    if local:
        for col in (EV_KA, EV_VA):
            in_specs += [
                pl.BlockSpec((None, A_WINDOW, LANES), lambda b, i, col=col: (b, jnp.maximum(i * r - 1, 0), col)),
                pl.BlockSpec((None, tq, LANES), lambda b, i, col=col: (b, i, col)),
                pl.BlockSpec((None, A_WINDOW, LANES),
                             lambda b, i, col=col: (b, jnp.minimum((i + 1) * r, nblk - 1), col)),
            ]
            args += [qsrc, qsrc, qsrc]
    in_specs += [pl.BlockSpec((None, lctx, LANES), lambda b, i: (b, 0, EV_KA)),
                 pl.BlockSpec((None, lctx, LANES), lambda b, i: (b, 0, EV_VA))]
    args += [pc, pc]
    return pl.pallas_call(
        functools.partial(_attn_a_kernel, local=local, tq=tq, seq=n),
        grid=(bsz, nq),
        in_specs=in_specs,
        out_specs=pl.BlockSpec((None, tq, qw), lambda b, i: (b, i, 0)),
        out_shape=jax.ShapeDtypeStruct((bsz, n, qw), BF16),
        compiler_params=pltpu.CompilerParams(
            dimension_semantics=("parallel", "parallel"), vmem_limit_bytes=VMEM_LIMIT),
        name="attn_a_lat" if local else "attn_a_ctx",
    )(*args)


def _attn_full_kernel(*refs, diff, has_lat, lam_init, tq):
    refs = list(refs)
    o_ref = refs.pop()
    if diff:
        lamv_ref = refs.pop(0)
        subln_ref = refs.pop()
    q_ref = refs.pop(0)
    if has_lat:
        kl_ref, vl_ref, kx_ref, vx_ref = refs
    else:
        kx_ref, vx_ref = refs
    q = q_ref[...]
    if diff:
        is_a = _lane_is_a((1, LANES))
        zero = jnp.zeros_like(q)
        qs = jnp.concatenate([jnp.where(is_a, q, zero), jnp.where(is_a, zero, q)], axis=0)
    else:
        qs = q
    s_x = _dot_nt(qs, kx_ref[...])
    m = jnp.max(s_x, axis=-1, keepdims=True)
    if has_lat:
        s_l = _dot_nt(qs, kl_ref[...])
        m = jnp.maximum(m, jnp.max(s_l, axis=-1, keepdims=True))
    e_x = jnp.exp(s_x - m)
    l = jnp.sum(e_x, axis=-1, keepdims=True)
    if has_lat:
        e_l = jnp.exp(s_l - m)
        l = l + jnp.sum(e_l, axis=-1, keepdims=True)
    if diff:
        lv = lamv_ref[...]
        lam = (jnp.exp(jnp.sum(lv[0:1] * lv[1:2], axis=-1, keepdims=True))
               - jnp.exp(jnp.sum(lv[2:3] * lv[3:4], axis=-1, keepdims=True)) + lam_init)
        c1 = 1.0 / l[:tq]
        c2 = -lam / l[tq:]
        o = _dot((e_x[:tq] * c1 + e_x[tq:] * c2).astype(BF16), vx_ref[...])
        if has_lat:
            o = o + _dot((e_l[:tq] * c1 + e_l[tq:] * c2).astype(BF16), vl_ref[...])
        ms = jnp.mean(o * o, axis=-1, keepdims=True)
        o = (o * lax.rsqrt(ms + EPS) * subln_ref[...]) * (1.0 - lam_init)
    else:
        o = _dot(e_x.astype(BF16), vx_ref[...])
        if has_lat:
            o = o + _dot(e_l.astype(BF16), vl_ref[...])
        o = o / l
    o_ref[...] = o.astype(BF16)


def _attn_full(qsrc, q_col, lat, ctx, k_col, v_col, nheads, tq, diff, lamv=None, subln=None, lam_init=0.0,
               name="attn_full"):
    bsz, n, _ = qsrc.shape
    in_specs, args = [], []
    if diff:
        in_specs.append(_const_spec(lamv.shape))
        args.append(lamv)
    in_specs.append(pl.BlockSpec((None, tq, LANES), lambda b, h, i: (b, i, q_col + h)))
    args.append(qsrc)
    for src in ([lat] if lat is not None else []) + [ctx]:
        nk = src.shape[1]
        in_specs += [pl.BlockSpec((None, nk, LANES), lambda b, h, i: (b, 0, k_col + h)),
                     pl.BlockSpec((None, nk, LANES), lambda b, h, i: (b, 0, v_col + h))]
        args += [src, src]
    if diff:
        in_specs.append(_const_spec(subln.shape))
        args.append(subln)
    return pl.pallas_call(
        functools.partial(_attn_full_kernel, diff=diff, has_lat=lat is not None, lam_init=lam_init, tq=tq),
        grid=(bsz, nheads, n // tq),
        in_specs=in_specs,
        out_specs=pl.BlockSpec((None, tq, LANES), lambda b, h, i: (b, i, h)),
        out_shape=jax.ShapeDtypeStruct((bsz, n, nheads * LANES), BF16),
        compiler_params=pltpu.CompilerParams(
            dimension_semantics=("parallel", "parallel", "parallel"), vmem_limit_bytes=VMEM_LIMIT),
        name=name,
    )(*args)


FFN_CHUNK = MXU_DIM


def _out_ffn_kernel(x_ref, ma_ref, mb_ref, mod_ref, g_ref, wo_ref, wi_ref, wo2_ref, o_ref):
    dm = D_MODEL
    mix = _dot(ma_ref[...], wo_ref[:MIX_HALF, :]) + _dot(mb_ref[...], wo_ref[MIX_HALF:, :])
    x1 = x_ref[...] + mod_ref[:, 2 * dm:3 * dm] * mix
    h2 = _modulated_norm(x1, g_ref[...], mod_ref[:, 3 * dm:4 * dm], mod_ref[:, 4 * dm:5 * dm]).astype(BF16)
    acc = jnp.zeros(x1.shape, F32)
    for c in range(FFN_HIDDEN // FFN_CHUNK):
        lo = c * FFN_CHUNK
        gate = _dot(h2, wi_ref[:, lo:lo + FFN_CHUNK])
        up = _dot(h2, wi_ref[:, FFN_HIDDEN + lo:FFN_HIDDEN + lo + FFN_CHUNK])
        act = (gate * jax.nn.sigmoid(gate) * up).astype(BF16)
        acc = acc + _dot(act, wo2_ref[lo:lo + FFN_CHUNK, :])
    o_ref[...] = x1 + mod_ref[:, 5 * dm:6 * dm] * acc


def _out_ffn(xs, ma, ma_col, mb, mods, mod_row, g, wo, wi, wo2, tm, name):
    bsz, n, _ = xs.shape
    return pl.pallas_call(
        _out_ffn_kernel,
        grid=(bsz, n // tm),
        in_specs=[
            pl.BlockSpec((None, tm, D_MODEL), lambda b, i: (b, i, 0)),
            pl.BlockSpec((None, tm, MIX_HALF), lambda b, i: (b, i, ma_col)),
            pl.BlockSpec((None, tm, MIX_HALF), lambda b, i: (b, i, 0)),
            pl.BlockSpec((None, 1, 6 * D_MODEL), lambda b, i: (mod_row(b), 0, 0)),
            _const_spec((1, D_MODEL)),
            _const_spec((D_MODEL, D_MODEL)),
            _const_spec((D_MODEL, 2 * FFN_HIDDEN)),
            _const_spec((FFN_HIDDEN, D_MODEL)),
        ],
        out_specs=pl.BlockSpec((None, tm, D_MODEL), lambda b, i: (b, i, 0)),
        out_shape=jax.ShapeDtypeStruct(xs.shape, F32),
        compiler_params=pltpu.CompilerParams(
            dimension_semantics=("parallel", "parallel"), vmem_limit_bytes=VMEM_LIMIT),
        name=name,
    )(xs, ma, mb, mods, g, wo, wi, wo2)


def _mla_keys_values(ckv, kpe_tile, kvan_ref, wukv_ref, m_ref, kgain_ref, cos, sin):
    ms = jnp.mean(ckv * ckv, axis=-1, keepdims=True)
    ckv_n = (ckv * lax.rsqrt(ms + EPS) * kvan_ref[...]).astype(BF16)
    kv = _dot(ckv_n, wukv_ref[...])
    nk = D_HEADS * LANES
    kraw = jnp.concatenate(
        [kv[:, h * LANES:(h + 1) * LANES] + kpe_tile for h in range(D_HEADS)], axis=1)
    ktiles = _norm_rope_tiles(kraw, m_ref, kgain_ref, D_HEADS, cos, sin)
    return ktiles, kv[:, nk:].astype(BF16)


def _proj_odd_kernel(x_ref, mod_ref, g_ref, w_ref, vn_ref, ws_ref, bs_ref, qan_ref, kvan_ref, wuq_ref,
                     wukv_ref, m_ref, qgain_ref, kgain_ref, cos_ref, sin_ref, o_ref, *, tm):
    h = _modulated_norm(x_ref[...], g_ref[...], mod_ref[:, 0:D_MODEL], mod_ref[:, D_MODEL:2 * D_MODEL])
    t = _dot(h.astype(BF16), w_ref[...])
    cos, sin = cos_ref[...], sin_ref[...]
    uv = t[:, :ODD_C]
    uv = 0.5 * uv * (1.0 + lax.erf(uv * math.sqrt(0.5)))
    u = uv[:, :C_WIDTH]
    v = uv[:, C_WIDTH:]
    ms = jnp.mean(v * v, axis=-1, keepdims=True)
    v = (v * lax.rsqrt(ms + EPS) * vn_ref[...]).astype(BF16)
    for c in range(tm // C_CHUNK):
        r0 = c * C_CHUNK
        for gidx in range(C_GROUPS):
            c0 = gidx * LANES
            vs = _dot(ws_ref[gidx], v[r0:r0 + C_CHUNK, c0:c0 + LANES]) + bs_ref[gidx]
            o_ref[r0:r0 + C_CHUNK, c0:c0 + LANES] = (u[r0:r0 + C_CHUNK, c0:c0 + LANES] * vs).astype(BF16)
    cq = t[:, ODD_C:ODD_C + D_Q_RANK]
    ms = jnp.mean(cq * cq, axis=-1, keepdims=True)
    cq_n = (cq * lax.rsqrt(ms + EPS) * qan_ref[...]).astype(BF16)
    qtiles = _norm_rope_tiles(_dot(cq_n, wuq_ref[...]), m_ref, qgain_ref, D_HEADS, cos, sin)
    for hh, tile in enumerate(qtiles):
        o_ref[:, (OD_Q + hh) * LANES:(OD_Q + hh + 1) * LANES] = tile
    k0 = ODD_C + D_Q_RANK
    ktiles, vals = _mla_keys_values(t[:, k0:k0 + D_KV_RANK], t[:, k0 + D_KV_RANK:], kvan_ref, wukv_ref,
                                    m_ref, kgain_ref, cos, sin)
    for hh, tile in enumerate(ktiles):
        o_ref[:, (OD_K + hh) * LANES:(OD_K + hh + 1) * LANES] = tile
    o_ref[:, OD_V * LANES:] = vals


def _proj_odd_ctx_kernel(x_ref, mod_ref, g_ref, w_ref, kvan_ref, wukv_ref, m_ref, kgain_ref, o_ref):
    h = _modulated_norm(x_ref[...], g_ref[...], mod_ref[:, 0:D_MODEL], mod_ref[:, D_MODEL:2 * D_MODEL])
    t = _dot(h.astype(BF16), w_ref[...])
    ktiles, vals = _mla_keys_values(t[:, :D_KV_RANK], t[:, D_KV_RANK:], kvan_ref, wukv_ref, m_ref, kgain_ref,
                                    None, None)
    for hh, tile in enumerate(ktiles):
        o_ref[:, (OD_K + hh) * LANES:(OD_K + hh + 1) * LANES] = tile
    o_ref[:, OD_V * LANES:] = vals


def _proj_odd(xs, mods, g, w, vn, ws, bs, qan, kvan, wuq, wukv, m256, qgain, kgain, cos, sin, tm):
    bsz, n, _ = xs.shape
    consts = [g, w, vn, ws, bs, qan, kvan, wuq, wukv, m256, qgain, kgain]
    return pl.pallas_call(
        functools.partial(_proj_odd_kernel, tm=tm),
        grid=(bsz, n // tm),
        in_specs=[pl.BlockSpec((None, tm, D_MODEL), lambda b, i: (b, i, 0)),
                  pl.BlockSpec((None, 1, 6 * D_MODEL), lambda b, i: (b, 0, 0))]
        + [_const_spec(a.shape) for a in consts]
        + [pl.BlockSpec((tm, LANES), lambda b, i: (i, 0))] * 2,
        out_specs=pl.BlockSpec((None, tm, OD_OUT), lambda b, i: (b, i, 0)),
        out_shape=jax.ShapeDtypeStruct((bsz, n, OD_OUT), BF16),
        compiler_params=pltpu.CompilerParams(
            dimension_semantics=("parallel", "parallel"), vmem_limit_bytes=VMEM_LIMIT),
        name="proj_odd_lat",
    )(xs, mods, *consts, cos, sin)


def _proj_odd_ctx(xs, mods, mod_row, g, w, kvan, wukv, m256, kgain, tm):
    bsz, n, _ = xs.shape
    consts = [g, w, kvan, wukv, m256, kgain]
    return pl.pallas_call(
        _proj_odd_ctx_kernel,
        grid=(bsz, n // tm),
        in_specs=[pl.BlockSpec((None, tm, D_MODEL), lambda b, i: (b, i, 0)),
                  pl.BlockSpec((None, 1, 6 * D_MODEL), lambda b, i: (mod_row(b), 0, 0))]
        + [_const_spec(a.shape) for a in consts],
        out_specs=pl.BlockSpec((None, tm, OD_OUT), lambda b, i: (b, i, 0)),
        out_shape=jax.ShapeDtypeStruct((bsz, n, OD_OUT), BF16),
        compiler_params=pltpu.CompilerParams(
            dimension_semantics=("parallel", "parallel"), vmem_limit_bytes=VMEM_LIMIT),
        name="proj_odd_ctx",
    )(xs, mods, *consts)


def _pair_cols(ca, cb):
    lo = np.arange(HALF)
    return np.concatenate([ca + lo, cb + lo, ca + HALF + lo, cb + HALF + lo])


def _pair_gain(ga, gb, scale=1.0):
    return jnp.concatenate([ga[:HALF], gb[:HALF], ga[HALF:], gb[HALF:]]).astype(F32) * scale


def _even_col_perm():
    a_q, b_q = 0, A_HEADS * HEAD_DIM
    a_k = 2 * MIX_HALF
    a_v = a_k + A_KV_HEADS * HEAD_DIM
    b_k = a_v + A_KV_HEADS * HEAD_DIM
    b_v = b_k + B_HEADS * 2 * HEAD_DIM
    grp = A_HEADS // A_KV_HEADS
    cols = [_pair_cols(a_q + j * HEAD_DIM, a_q + (grp + j) * HEAD_DIM) for j in range(grp)]
    cols += [_pair_cols(b_q + h * LANES, b_q + h * LANES + HEAD_DIM) for h in range(B_HEADS)]
    cols += [_pair_cols(a_k, a_k + HEAD_DIM)]
    cols += [_pair_cols(b_k + h * LANES, b_k + h * LANES + HEAD_DIM) for h in range(B_HEADS)]
    cols += [np.arange(a_v, b_k), np.arange(b_v, EVEN_IN)]
    return np.concatenate(cols)


def _even_out_row_perm():
    grp = A_HEADS // A_KV_HEADS
    rows = []
    for j in range(grp):
        rows += [np.arange(j * HEAD_DIM, (j + 1) * HEAD_DIM),
                 np.arange((grp + j) * HEAD_DIM, (grp + j + 1) * HEAD_DIM)]
    rows.append(np.arange(MIX_HALF, D_MODEL))
    return np.concatenate(rows)


def _block_mean_matrix():
    seg = (np.arange(MXU_DIM) // LANES) * 2 + (np.arange(MXU_DIM) // HALF) % 2
    return jnp.asarray((seg[:, None] == seg[None, :]).astype(np.float32) / HEAD_DIM, dtype=BF16)


def _rope_tables(n_tokens):
    rows = n_tokens // GRID_W
    row = jnp.repeat(jnp.arange(rows, dtype=jnp.int32), GRID_W).astype(F32)
    col = jnp.tile(jnp.arange(GRID_W, dtype=jnp.int32), rows).astype(F32)
    inv = ROPE_THETA ** (-jnp.arange(ROPE_AXIS_PAIRS, dtype=F32) / ROPE_AXIS_PAIRS)
    ang = jnp.concatenate([row[:, None] * inv, col[:, None] * inv], axis=-1)
    c, s = jnp.cos(ang), jnp.sin(ang)
    one, zero = jnp.ones_like(c), jnp.zeros_like(c)
    even = (jnp.concatenate([c, c, c, c], axis=1), jnp.concatenate([-s, -s, s, s], axis=1))
    odd = (jnp.concatenate([one, c, one, c], axis=1), jnp.concatenate([zero, -s, zero, s], axis=1))
    return even, odd


def _pad_rows(a, rows):
    return jnp.concatenate([a, jnp.zeros((rows - a.shape[0],) + a.shape[1:], a.dtype)], axis=0)


def kernel(x, c, ctx, c_ctx, norm1_g, norm2_g, ada_w, ada_b, mix_w_out, ffn_w_in, ffn_w_out, ev_w_in, ev_qnorm_a, ev_knorm_a, ev_sink, ev_qnorm_b, ev_knorm_b, ev_lam_q1, ev_lam_k1, ev_lam_q2, ev_lam_k2, ev_subln, od_w_in, od_c_vnorm, od_c_ws, od_c_bs, od_qa_norm, od_kva_norm, od_w_uq, od_w_ukv, od_qnorm_nope, od_knorm_nope, od_qnorm_rope, od_knorm_rope):
    bsz, seq, dm = x.shape
    lctx = ctx.shape[1]
    assert (dm, DEPTH) == (D_MODEL, 2) and seq % 512 == 0 and lctx % 256 == 0
    ctx_row = bsz

    cv = _pad_rows(jnp.concatenate([c, c_ctx[None, :]], axis=0), 16)
    mods = _adaln(cv, ada_w, ada_b).reshape(DEPTH, 16, 1, 6 * dm)
    (cos_e, sin_e), (cos_o, sin_o) = _rope_tables(seq)
    m256 = _block_mean_matrix()
    lat_row = lambda b: b
    ctx_mod = lambda b: ctx_row

    w_in0 = ev_w_in[0][:, _even_col_perm()].astype(BF16)
    sa = HEAD_DIM ** -0.5
    gains0 = _pad_rows(jnp.stack(
        [_pair_gain(ev_qnorm_a[0], ev_qnorm_a[0], sa)] * 4 + [_pair_gain(ev_qnorm_b[0], ev_qnorm_b[0], sa)] * 4
        + [_pair_gain(ev_knorm_a[0], ev_knorm_a[0])] + [_pair_gain(ev_knorm_b[0], ev_knorm_b[0])] * 4), 16)
    g1 = norm1_g[0][None, :]
    pl0 = _proj_even(x, mods[0], lat_row, g1, w_in0, m256, gains0, cos_e, sin_e, tm=512)
    pc0 = _proj_even(ctx, mods[0], ctx_mod, g1, w_in0, m256, gains0, None, None, tm=lctx)

    lamv = _pad_rows(jnp.stack([ev_lam_q1[0], ev_lam_k1[0], ev_lam_q2[0], ev_lam_k2[0]]), 8)
    subln = ev_subln[0][None, :]
    lam_init = 0.8 - 0.6 * math.exp(-0.3 * 0)
    sink = ev_sink[0].astype(F32)
    out_a = _attn_a(pl0, pc0, sink, local=True, tq=A_WINDOW)
    out_ac = _attn_a(pc0, pc0, sink, local=False, tq=lctx)
    out_b = _attn_full(pl0, EV_QB, pl0, pc0, EV_KB, EV_VB, B_HEADS, 256, True, lamv, subln, lam_init,
                       name="attn_b_lat")
    out_bc = _attn_full(pc0, EV_QB, None, pc0, EV_KB, EV_VB, B_HEADS, lctx, True, lamv, subln, lam_init,
                        name="attn_b_ctx")

    wo0 = mix_w_out[0][_even_out_row_perm(), :].astype(BF16)
    wi0 = ffn_w_in[0].astype(BF16)
    wo20 = ffn_w_out[0].astype(BF16)
    g2 = norm2_g[0][None, :]
    x1 = _out_ffn(x, out_a, 0, out_b, mods[0], lat_row, g2, wo0, wi0, wo20, 512, "out_ffn0_lat")
    xc1 = _out_ffn(ctx, out_ac, 0, out_bc, mods[0], ctx_mod, g2, wo0, wi0, wo20, lctx, "out_ffn0_ctx")

    w1 = od_w_in[0]
    z32 = jnp.zeros((dm, HALF), F32)
    kv0 = ODD_C + D_Q_RANK
    pe0 = kv0 + D_KV_RANK
    kv_cols = jnp.concatenate([w1[:, kv0:pe0], z32, w1[:, pe0:pe0 + HALF], z32, w1[:, pe0 + HALF:]], axis=1)
    w_in1 = jnp.concatenate([w1[:, :kv0], kv_cols], axis=1).astype(BF16)
    w_in1_ctx = kv_cols.astype(BF16)
    wuq = od_w_uq[0][:, np.concatenate(
        [_pair_cols(h * LANES, h * LANES + D_NOPE) for h in range(D_HEADS)])].astype(BF16)
    per_head = D_NOPE + D_VDIM
    zcol = D_HEADS * per_head
    lo = np.arange(HALF)
    kcols = np.concatenate([np.concatenate([h * per_head + lo, np.full(HALF, zcol), h * per_head + HALF + lo,
                                            np.full(HALF, zcol)]) for h in range(D_HEADS)])
    vcols = np.concatenate([h * per_head + D_NOPE + np.arange(D_VDIM) for h in range(D_HEADS)])
    wukv = jnp.concatenate([od_w_ukv[0], jnp.zeros((D_KV_RANK, 1), F32)], axis=1)[
        :, np.concatenate([kcols, vcols])].astype(BF16)
    sd = (D_NOPE + D_ROPE) ** -0.5
    qgain = _pad_rows(jnp.stack([_pair_gain(od_qnorm_nope[0], od_qnorm_rope[0], sd)] * D_HEADS), 8)
    kgain = _pad_rows(jnp.stack([_pair_gain(od_knorm_nope[0], od_knorm_rope[0])] * D_HEADS), 8)
    ws = od_c_ws[0].astype(BF16)
    bs = jnp.broadcast_to(od_c_bs[0][:, :, None], (C_GROUPS, C_CHUNK, LANES)).astype(F32)
    g1 = norm1_g[1][None, :]
    kvan = od_kva_norm[0][None, :]
    pl1 = _proj_odd(x1, mods[1], g1, w_in1, od_c_vnorm[0][None, :], ws, bs, od_qa_norm[0][None, :], kvan,
                    wuq, wukv, m256, qgain, kgain, cos_o, sin_o, tm=512)
    pc1 = _proj_odd_ctx(xc1, mods[1], ctx_mod, g1, w_in1_ctx, kvan, wukv, m256, kgain, tm=lctx)
    out_d = _attn_full(pl1, OD_Q, pl1, pc1, OD_K, OD_V, D_HEADS, 256, False, name="attn_d_lat")

    return _out_ffn(x1, pl1, 0, out_d, mods[1], lat_row, norm2_g[1][None, :], mix_w_out[1].astype(BF16),
                    ffn_w_in[1].astype(BF16), ffn_w_out[1].astype(BF16), 512, "out_ffn1_lat")
```

```python
import functools
import math

import numpy as np
import jax
import jax.numpy as jnp
from jax import lax
from jax.experimental import pallas as pl
from jax.experimental.pallas import tpu as pltpu

F32 = jnp.float32
BF16 = jnp.bfloat16

D_MODEL = 1024
DEPTH = 2
GRID_W = 64
HEAD_DIM = 64
HALF = HEAD_DIM // 2
ROPE_THETA = 10000.0
ROPE_AXIS_PAIRS = HEAD_DIM // 4
EPS = 1e-6
NEG = -1e30
MIX_HALF = D_MODEL // 2
A_HEADS = 8
A_KV_HEADS = 2
A_WINDOW = 128
B_HEADS = 4
B_VDIM = 128
C_WIDTH = MIX_HALF
C_GROUPS = 4
C_CHUNK = 128
D_NOPE = 64
D_ROPE = 64
D_VDIM = 128
D_HEADS = 4
D_Q_RANK = 256
D_KV_RANK = 256
FFN_HIDDEN = 2816
EVEN_IN = 2304
ODD_C = 2 * C_WIDTH

LANES = 128
MXU_DIM = 256
VMEM_LIMIT = 56 * 1024 * 1024
ATTN_KEY_CHUNK = 512
LOG2E = math.log2(math.e)

EV_QA, EV_QB, EV_KA, EV_KB, EV_VA, EV_VB = 0, 4, 8, 9, 13, 14
EV_NORM_TILES = 13
OD_C, OD_Q, OD_K, OD_V = 0, 4, 8, 12
OD_OUT = 2048
OD_IN_PAD = 1664


def _dot(a, b):
    return jnp.dot(a, b, preferred_element_type=F32)


def _dot_nt(a, b):
    return lax.dot_general(a, b, (((1,), (1,)), ((), ())), preferred_element_type=F32)


def _lane_is_a(shape):
    lane = lax.broadcasted_iota(jnp.int32, shape, len(shape) - 1)
    return (lane // HALF) % 2 == 0


def _modulated_norm(x, g, shift, scale):
    ms = jnp.mean(x * x, axis=-1, keepdims=True)
    return (x * lax.rsqrt(ms + EPS) * g) * (1.0 + scale) + shift


def _norm_rope_tiles(t, m_ref, gains_ref, n_tiles, cos, sin):
    out = []
    j = 0
    while j < n_tiles:
        width = MXU_DIM if j + 1 < n_tiles else LANES
        tp = t[:, j * LANES:j * LANES + width]
        msq = _dot((tp * tp).astype(BF16), m_ref[:width, :width])
        for q in range(width // LANES):
            tt = tp[:, q * LANES:(q + 1) * LANES]
            r = lax.rsqrt(msq[:, q * LANES:(q + 1) * LANES] + EPS)
            tn = tt * r * gains_ref[j + q:j + q + 1, :]
            if cos is not None:
                tn = tn * cos + pltpu.roll(tn, 2 * HALF, 1) * sin
            out.append(tn.astype(BF16))
        j += width // LANES
    return out


def _adaln_kernel(cv_ref, w_ref, b_ref, o_ref):
    a = cv_ref[...]
    a = a * jax.nn.sigmoid(a)
    w = w_ref[...]
    a_hi = a.astype(BF16)
    a_lo = (a - a_hi.astype(F32)).astype(BF16)
    w_hi = w.astype(BF16)
    w_lo = (w - w_hi.astype(F32)).astype(BF16)
    o_ref[...] = _dot(a_hi, w_hi) + _dot(a_hi, w_lo) + _dot(a_lo, w_hi) + b_ref[...]


def _adaln(cv, ada_w, ada_b):
    tn = 768
    n6 = 6 * D_MODEL
    rows = cv.shape[0]
    return pl.pallas_call(
        _adaln_kernel,
        grid=(DEPTH, n6 // tn),
        in_specs=[
            pl.BlockSpec((rows, D_MODEL), lambda l, j: (0, 0)),
            pl.BlockSpec((None, D_MODEL, tn), lambda l, j: (l, 0, j)),
            pl.BlockSpec((None, 1, tn), lambda l, j: (l, 0, j)),
        ],
        out_specs=pl.BlockSpec((None, rows, tn), lambda l, j: (l, 0, j)),
        out_shape=jax.ShapeDtypeStruct((DEPTH, rows, n6), F32),
        compiler_params=pltpu.CompilerParams(
            dimension_semantics=("arbitrary", "arbitrary"), vmem_limit_bytes=VMEM_LIMIT),
        name="adaln",
    )(cv, ada_w, ada_b.reshape(DEPTH, 1, n6))


def _proj_even_kernel(*refs, rope):
    if rope:
        x_ref, mod_ref, g_ref, w_ref, m_ref, gains_ref, cos_ref, sin_ref, o_ref = refs
        cos, sin = cos_ref[...], sin_ref[...]
    else:
        x_ref, mod_ref, g_ref, w_ref, m_ref, gains_ref, o_ref = refs
        cos = sin = None
    h = _modulated_norm(x_ref[...], g_ref[...], mod_ref[:, 0:D_MODEL], mod_ref[:, D_MODEL:2 * D_MODEL])
    t = _dot(h.astype(BF16), w_ref[...])
    tiles = _norm_rope_tiles(t, m_ref, gains_ref, EV_NORM_TILES, cos, sin)
    for j, tile in enumerate(tiles):
        o_ref[:, j * LANES:(j + 1) * LANES] = tile
    o_ref[:, EV_NORM_TILES * LANES:] = t[:, EV_NORM_TILES * LANES:].astype(BF16)


def _const_spec(shape):
    nd = len(shape)
    return pl.BlockSpec(shape, lambda *_: (0,) * nd, pipeline_mode=pl.Buffered(1))


def _proj_even(xs, mods, mod_row, g, w, m256, gains, cos, sin, tm):
    bsz, n, _ = xs.shape
    rope = cos is not None
    in_specs = [
        pl.BlockSpec((None, tm, D_MODEL), lambda b, i: (b, i, 0)),
        pl.BlockSpec((None, 1, 6 * D_MODEL), lambda b, i: (mod_row(b), 0, 0)),
        _const_spec((1, D_MODEL)),
        _const_spec((D_MODEL, EVEN_IN)),
        _const_spec((MXU_DIM, MXU_DIM)),
        _const_spec(gains.shape),
    ]
    args = [xs, mods, g, w, m256, gains]
    if rope:
        in_specs += [pl.BlockSpec((tm, LANES), lambda b, i: (i, 0))] * 2
        args += [cos, sin]
    return pl.pallas_call(
        functools.partial(_proj_even_kernel, rope=rope),
        grid=(bsz, n // tm),
        in_specs=in_specs,
        out_specs=pl.BlockSpec((None, tm, EVEN_IN), lambda b, i: (b, i, 0)),
        out_shape=jax.ShapeDtypeStruct((bsz, n, EVEN_IN), BF16),
        compiler_params=pltpu.CompilerParams(
            dimension_semantics=("parallel", "parallel"), vmem_limit_bytes=VMEM_LIMIT),
        name="proj_even_lat" if rope else "proj_even_ctx",
    )(*args)


def _attn_a_kernel(*refs, local, tq):
    grp = A_HEADS // A_KV_HEADS
    if local:
        sink_ref, q_ref, kp_ref, kc_ref, kn_ref, vp_ref, vc_ref, vn_ref, band_ref, kx_ref, vx_ref, o_ref = refs
        k_loc = jnp.concatenate([kp_ref[...], kc_ref[...], kn_ref[...]], axis=0)
        v_loc = jnp.concatenate([vp_ref[...], vc_ref[...], vn_ref[...]], axis=0)
        tk = tq + 2 * A_WINDOW
        i = pl.program_id(1)
        neg_first = jnp.where(i == 0, NEG, 0.0)
        neg_last = jnp.where(i == pl.num_programs(1) - 1, NEG, 0.0)
        col = lax.broadcasted_iota(jnp.int32, (1, tk), 1)
        bias = (band_ref[...] + jnp.where(col < A_WINDOW, neg_first, 0.0)
                + jnp.where(col >= tq + A_WINDOW, neg_last, 0.0))
    else:
        sink_ref, q_ref, kx_ref, vx_ref, o_ref = refs
    q = q_ref[...]
    qs = jnp.concatenate([q[:, j * LANES:(j + 1) * LANES] for j in range(grp)], axis=0)
    kx = kx_ref[...]
    vx = vx_ref[...]
    is_a = _lane_is_a((1, LANES))
    outs = []
    for kvh in range(A_KV_HEADS):
        sel = is_a if kvh == 0 else jnp.logical_not(is_a)
        s_ctx = _dot_nt(qs, jnp.where(sel, kx, jnp.zeros_like(kx)))
        sink = jnp.concatenate(
            [jnp.full((tq, 1), sink_ref[kvh * grp + j] * LOG2E, F32) for j in range(grp)], axis=0)
        m_part = _lane_fold(s_ctx, jnp.maximum)
        if local:
            s_loc = _dot_nt(qs, jnp.where(sel, k_loc, jnp.zeros_like(k_loc)))
            s_loc = (s_loc.reshape(grp, tq, tk) + bias[None]).reshape(grp * tq, tk)
            m_part = jnp.maximum(m_part, _lane_fold(s_loc, jnp.maximum))
        m = jnp.maximum(jnp.max(m_part, axis=-1, keepdims=True), sink)
        e_ctx = jnp.exp2(s_ctx - m)
        l_part = _lane_fold(e_ctx, jnp.add)
        o = _dot(e_ctx.astype(BF16), vx)
        if local:
            e_loc = jnp.exp2(s_loc - m)
            l_part = l_part + _lane_fold(e_loc, jnp.add)
            o = o + _dot(e_loc.astype(BF16), v_loc)
        l = jnp.sum(l_part, axis=-1, keepdims=True) + jnp.exp2(sink - m)
        outs.append(o / l)
    lane = lax.broadcasted_iota(jnp.int32, (1, LANES), 1)
    res = jnp.where(lane < HEAD_DIM, outs[0], outs[1])
    for j in range(grp):
        o_ref[:, j * LANES:(j + 1) * LANES] = res[j * tq:(j + 1) * tq].astype(BF16)


def _attn_a(qsrc, pc, sink, local, tq):
    bsz, n, _ = qsrc.shape
    lctx = pc.shape[1]
    nq = n // tq
    r = tq // A_WINDOW
    nblk = n // A_WINDOW
    qw = MIX_HALF
    in_specs = [pl.BlockSpec(memory_space=pltpu.SMEM),
                pl.BlockSpec((None, tq, qw), lambda b, i: (b, i, 0))]
    args = [sink, qsrc]
    if local:
        for col in (EV_KA, EV_VA):
            in_specs += [
                pl.BlockSpec((None, A_WINDOW, LANES), lambda b, i, col=col: (b, jnp.maximum(i * r - 1, 0), col)),
                pl.BlockSpec((None, tq, LANES), lambda b, i, col=col: (b, i, col)),
                pl.BlockSpec((None, A_WINDOW, LANES),
                             lambda b, i, col=col: (b, jnp.minimum((i + 1) * r, nblk - 1), col)),
            ]
            args += [qsrc, qsrc, qsrc]
        off = np.arange(tq + 2 * A_WINDOW)[None, :] - A_WINDOW
        band = np.where(np.abs(np.arange(tq)[:, None] - off) <= A_WINDOW, 0.0, NEG).astype(np.float32)
        in_specs.append(_const_spec(band.shape))
        args.append(jnp.asarray(band))
    in_specs += [pl.BlockSpec((None, lctx, LANES), lambda b, i: (b, 0, EV_KA)),
                 pl.BlockSpec((None, lctx, LANES), lambda b, i: (b, 0, EV_VA))]
    args += [pc, pc]
    return pl.pallas_call(
        functools.partial(_attn_a_kernel, local=local, tq=tq),
        grid=(bsz, nq),
        in_specs=in_specs,
        out_specs=pl.BlockSpec((None, tq, qw), lambda b, i: (b, i, 0)),
        out_shape=jax.ShapeDtypeStruct((bsz, n, qw), BF16),
        compiler_params=pltpu.CompilerParams(
            dimension_semantics=("parallel", "parallel"), vmem_limit_bytes=VMEM_LIMIT),
        name="attn_a_lat" if local else "attn_a_ctx",
    )(*args)


def _lane_fold(x, op):
    out = x[:, :LANES]
    for t in range(1, x.shape[1] // LANES):
        out = op(out, x[:, t * LANES:(t + 1) * LANES])
    return out


def _attn_full_kernel(*refs, diff, has_lat, lam_init, tq, ck):
    refs = list(refs)
    m_scr = refs.pop()
    s_scr = refs.pop()
    o_ref = refs.pop()
    if diff:
        lamv_ref = refs.pop(0)
        subln_ref = refs.pop()
    q_ref = refs.pop(0)
    srcs = [(refs[0], refs[1]), (refs[2], refs[3])] if has_lat else [(refs[0], refs[1])]

    @pl.when(pl.program_id(0) == 0)
    def _():
        s_scr[...] = jnp.zeros(s_scr.shape, F32)
        m_scr[...] = jnp.zeros(m_scr.shape, F32)

    q = q_ref[...]
    if diff:
        is_a = _lane_is_a((1, LANES))
        zero = jnp.zeros_like(q)
        qs = jnp.concatenate([jnp.where(is_a, q, zero), jnp.where(is_a, zero, q)], axis=0)
    else:
        qs = q
    m_prev = m_scr[...]
    acc = l_part = m_part = None
    col = 0
    for k_ref, v_ref in srcs:
        nk = k_ref.shape[0]
        for st in range(0, nk, ck):
            sz = min(ck, nk - st)
            s_old = s_scr[:, col:col + sz]
            e = jnp.exp2(jnp.concatenate(
                [s_old[:, t * LANES:(t + 1) * LANES] - m_prev for t in range(sz // LANES)], axis=1))
            lc = _lane_fold(e, jnp.add)
            l_part = lc if l_part is None else l_part + lc
            pv = _dot(e.astype(BF16), v_ref[st:st + sz, :])
            acc = pv if acc is None else acc + pv
            s_new = _dot_nt(qs, k_ref[st:st + sz, :])
            s_scr[:, col:col + sz] = s_new
            mc = _lane_fold(s_new, jnp.maximum)
            m_part = mc if m_part is None else jnp.maximum(m_part, mc)
            col += sz
    m_scr[...] = jnp.broadcast_to(jnp.max(m_part, axis=-1, keepdims=True), m_scr.shape)
    o = acc / jnp.sum(l_part, axis=-1, keepdims=True)
    if diff:
        lv = lamv_ref[...]
        lam = (jnp.exp(jnp.sum(lv[0:1] * lv[1:2], axis=-1, keepdims=True))
               - jnp.exp(jnp.sum(lv[2:3] * lv[3:4], axis=-1, keepdims=True)) + lam_init)
        o = o[:tq] - lam * o[tq:]
        ms = jnp.mean(o * o, axis=-1, keepdims=True)
        o = (o * lax.rsqrt(ms + EPS) * subln_ref[...]) * (1.0 - lam_init)
    o_ref[...] = o.astype(BF16)


def _attn_full(qsrc, q_col, lat, ctx, k_col, v_col, nheads, tq, diff, lamv=None, subln=None, lam_init=0.0,
               name="attn_full"):
    bsz, n, _ = qsrc.shape
    nq = n // tq
    n_tiles = bsz * nheads * nq

    def tile_of(j):
        return j // (nheads * nq), (j // nq) % nheads, j % nq

    def cur(j):
        return tile_of(jnp.minimum(j, n_tiles - 1))

    def prev(j):
        return tile_of(jnp.maximum(j - 1, 0))

    def q_map(j):
        b, h, i = cur(j)
        return b, i, q_col + h

    def k_map(j):
        b, h, _ = cur(j)
        return b, 0, k_col + h

    def v_map(j):
        b, h, _ = prev(j)
        return b, 0, v_col + h

    def o_map(j):
        b, h, i = prev(j)
        return b, i, h

    in_specs, args = [], []
    if diff:
        in_specs.append(_const_spec(lamv.shape))
        args.append(lamv)
    in_specs.append(pl.BlockSpec((None, tq, LANES), q_map))
    args.append(qsrc)
    for src in ([lat] if lat is not None else []) + [ctx]:
        nk = src.shape[1]
        in_specs += [pl.BlockSpec((None, nk, LANES), k_map), pl.BlockSpec((None, nk, LANES), v_map)]
        args += [src, src]
    if diff:
        in_specs.append(_const_spec(subln.shape))
        args.append(subln)
    rows = 2 * tq if diff else tq
    n_keys = ctx.shape[1] + (lat.shape[1] if lat is not None else 0)
    return pl.pallas_call(
        functools.partial(_attn_full_kernel, diff=diff, has_lat=lat is not None, lam_init=lam_init, tq=tq,
                          ck=ATTN_KEY_CHUNK),
        grid=(n_tiles + 1,),
        in_specs=in_specs,
        out_specs=pl.BlockSpec((None, tq, LANES), o_map),
        out_shape=jax.ShapeDtypeStruct((bsz, n, nheads * LANES), BF16),
        scratch_shapes=[pltpu.VMEM((rows, n_keys), F32), pltpu.VMEM((rows, LANES), F32)],
        compiler_params=pltpu.CompilerParams(
            dimension_semantics=("arbitrary",), vmem_limit_bytes=VMEM_LIMIT),
        name=name,
    )(*args)


FFN_CHUNK = MXU_DIM


def _out_ffn_kernel(x_ref, ma_ref, mb_ref, mod_ref, g_ref, wo_ref, wi_ref, wo2_ref, o_ref):
    dm = D_MODEL
    mix = _dot(ma_ref[...], wo_ref[:MIX_HALF, :]) + _dot(mb_ref[...], wo_ref[MIX_HALF:, :])
    x1 = x_ref[...] + mod_ref[:, 2 * dm:3 * dm] * mix
    h2 = _modulated_norm(x1, g_ref[...], mod_ref[:, 3 * dm:4 * dm], mod_ref[:, 4 * dm:5 * dm]).astype(BF16)
    acc = jnp.zeros(x1.shape, F32)
    for c in range(FFN_HIDDEN // FFN_CHUNK):
        lo = c * FFN_CHUNK
        gate = _dot(h2, wi_ref[:, lo:lo + FFN_CHUNK])
        up = _dot(h2, wi_ref[:, FFN_HIDDEN + lo:FFN_HIDDEN + lo + FFN_CHUNK])
        act = (gate * jax.nn.sigmoid(gate) * up).astype(BF16)
        acc = acc + _dot(act, wo2_ref[lo:lo + FFN_CHUNK, :])
    o_ref[...] = x1 + mod_ref[:, 5 * dm:6 * dm] * acc


def _out_ffn(xs, ma, ma_col, mb, mods, mod_row, g, wo, wi, wo2, tm, name):
    bsz, n, _ = xs.shape
    return pl.pallas_call(
        _out_ffn_kernel,
        grid=(bsz, n // tm),
        in_specs=[
            pl.BlockSpec((None, tm, D_MODEL), lambda b, i: (b, i, 0)),
            pl.BlockSpec((None, tm, MIX_HALF), lambda b, i: (b, i, ma_col)),
            pl.BlockSpec((None, tm, MIX_HALF), lambda b, i: (b, i, 0)),
            pl.BlockSpec((None, 1, 6 * D_MODEL), lambda b, i: (mod_row(b), 0, 0)),
            _const_spec((1, D_MODEL)),
            _const_spec((D_MODEL, D_MODEL)),
            _const_spec((D_MODEL, 2 * FFN_HIDDEN)),
            _const_spec((FFN_HIDDEN, D_MODEL)),
        ],
        out_specs=pl.BlockSpec((None, tm, D_MODEL), lambda b, i: (b, i, 0)),
        out_shape=jax.ShapeDtypeStruct(xs.shape, F32),
        compiler_params=pltpu.CompilerParams(
            dimension_semantics=("parallel", "parallel"), vmem_limit_bytes=VMEM_LIMIT),
        name=name,
    )(xs, ma, mb, mods, g, wo, wi, wo2)


def _mla_keys_values(ckv, kpe_tile, kvan_ref, wukv_ref, m_ref, kgain_ref, cos, sin):
    ms = jnp.mean(ckv * ckv, axis=-1, keepdims=True)
    ckv_n = (ckv * lax.rsqrt(ms + EPS) * kvan_ref[...]).astype(BF16)
    kv = _dot(ckv_n, wukv_ref[...])
    nk = D_HEADS * LANES
    kraw = jnp.concatenate(
        [kv[:, h * LANES:(h + 1) * LANES] + kpe_tile for h in range(D_HEADS)], axis=1)
    ktiles = _norm_rope_tiles(kraw, m_ref, kgain_ref, D_HEADS, cos, sin)
    return ktiles, kv[:, nk:].astype(BF16)


def _proj_odd_kernel(x_ref, mod_ref, g_ref, w_ref, vn_ref, ws_ref, bs_ref, qan_ref, kvan_ref, wuq_ref,
                     wukv_ref, m_ref, qgain_ref, kgain_ref, cos_ref, sin_ref, o_ref, *, tm):
    h = _modulated_norm(x_ref[...], g_ref[...], mod_ref[:, 0:D_MODEL], mod_ref[:, D_MODEL:2 * D_MODEL])
    t = _dot(h.astype(BF16), w_ref[...])
    cos, sin = cos_ref[...], sin_ref[...]
    uv = t[:, :ODD_C]
    uv = 0.5 * uv * (1.0 + lax.erf(uv * math.sqrt(0.5)))
    u = uv[:, :C_WIDTH]
    v = uv[:, C_WIDTH:]
    ms = jnp.mean(v * v, axis=-1, keepdims=True)
    v = (v * lax.rsqrt(ms + EPS) * vn_ref[...]).astype(BF16)
    for c in range(tm // C_CHUNK):
        r0 = c * C_CHUNK
        for gidx in range(C_GROUPS):
            c0 = gidx * LANES
            vs = _dot(ws_ref[gidx], v[r0:r0 + C_CHUNK, c0:c0 + LANES]) + bs_ref[gidx]
            o_ref[r0:r0 + C_CHUNK, c0:c0 + LANES] = (u[r0:r0 + C_CHUNK, c0:c0 + LANES] * vs).astype(BF16)
    cq = t[:, ODD_C:ODD_C + D_Q_RANK]
    ms = jnp.mean(cq * cq, axis=-1, keepdims=True)
    cq_n = (cq * lax.rsqrt(ms + EPS) * qan_ref[...]).astype(BF16)
    qtiles = _norm_rope_tiles(_dot(cq_n, wuq_ref[...]), m_ref, qgain_ref, D_HEADS, cos, sin)
    for hh, tile in enumerate(qtiles):
        o_ref[:, (OD_Q + hh) * LANES:(OD_Q + hh + 1) * LANES] = tile
    k0 = ODD_C + D_Q_RANK
    ktiles, vals = _mla_keys_values(t[:, k0:k0 + D_KV_RANK], t[:, k0 + D_KV_RANK:], kvan_ref, wukv_ref,
                                    m_ref, kgain_ref, cos, sin)
    for hh, tile in enumerate(ktiles):
        o_ref[:, (OD_K + hh) * LANES:(OD_K + hh + 1) * LANES] = tile
    o_ref[:, OD_V * LANES:] = vals


def _proj_odd_ctx_kernel(x_ref, mod_ref, g_ref, w_ref, kvan_ref, wukv_ref, m_ref, kgain_ref, o_ref):
    h = _modulated_norm(x_ref[...], g_ref[...], mod_ref[:, 0:D_MODEL], mod_ref[:, D_MODEL:2 * D_MODEL])
    t = _dot(h.astype(BF16), w_ref[...])
    ktiles, vals = _mla_keys_values(t[:, :D_KV_RANK], t[:, D_KV_RANK:], kvan_ref, wukv_ref, m_ref, kgain_ref,
                                    None, None)
    for hh, tile in enumerate(ktiles):
        o_ref[:, (OD_K + hh) * LANES:(OD_K + hh + 1) * LANES] = tile
    o_ref[:, OD_V * LANES:] = vals


def _proj_odd(xs, mods, g, w, vn, ws, bs, qan, kvan, wuq, wukv, m256, qgain, kgain, cos, sin, tm):
    bsz, n, _ = xs.shape
    consts = [g, w, vn, ws, bs, qan, kvan, wuq, wukv, m256, qgain, kgain]
    return pl.pallas_call(
        functools.partial(_proj_odd_kernel, tm=tm),
        grid=(bsz, n // tm),
        in_specs=[pl.BlockSpec((None, tm, D_MODEL), lambda b, i: (b, i, 0)),
                  pl.BlockSpec((None, 1, 6 * D_MODEL), lambda b, i: (b, 0, 0))]
        + [_const_spec(a.shape) for a in consts]
        + [pl.BlockSpec((tm, LANES), lambda b, i: (i, 0))] * 2,
        out_specs=pl.BlockSpec((None, tm, OD_OUT), lambda b, i: (b, i, 0)),
        out_shape=jax.ShapeDtypeStruct((bsz, n, OD_OUT), BF16),
        compiler_params=pltpu.CompilerParams(
            dimension_semantics=("parallel", "parallel"), vmem_limit_bytes=VMEM_LIMIT),
        name="proj_odd_lat",
    )(xs, mods, *consts, cos, sin)


def _proj_odd_ctx(xs, mods, mod_row, g, w, kvan, wukv, m256, kgain, tm):
    bsz, n, _ = xs.shape
    consts = [g, w, kvan, wukv, m256, kgain]
    return pl.pallas_call(
        _proj_odd_ctx_kernel,
        grid=(bsz, n // tm),
        in_specs=[pl.BlockSpec((None, tm, D_MODEL), lambda b, i: (b, i, 0)),
                  pl.BlockSpec((None, 1, 6 * D_MODEL), lambda b, i: (mod_row(b), 0, 0))]
        + [_const_spec(a.shape) for a in consts],
        out_specs=pl.BlockSpec((None, tm, OD_OUT), lambda b, i: (b, i, 0)),
        out_shape=jax.ShapeDtypeStruct((bsz, n, OD_OUT), BF16),
        compiler_params=pltpu.CompilerParams(
            dimension_semantics=("parallel", "parallel"), vmem_limit_bytes=VMEM_LIMIT),
        name="proj_odd_ctx",
    )(xs, mods, *consts)


def _pair_cols(ca, cb):
    lo = np.arange(HALF)
    return np.concatenate([ca + lo, cb + lo, ca + HALF + lo, cb + HALF + lo])


def _pair_gain(ga, gb, scale=1.0):
    return jnp.concatenate([ga[:HALF], gb[:HALF], ga[HALF:], gb[HALF:]]).astype(F32) * scale


def _even_col_perm():
    a_q, b_q = 0, A_HEADS * HEAD_DIM
    a_k = 2 * MIX_HALF
    a_v = a_k + A_KV_HEADS * HEAD_DIM
    b_k = a_v + A_KV_HEADS * HEAD_DIM
    b_v = b_k + B_HEADS * 2 * HEAD_DIM
    grp = A_HEADS // A_KV_HEADS
    cols = [_pair_cols(a_q + j * HEAD_DIM, a_q + (grp + j) * HEAD_DIM) for j in range(grp)]
    cols += [_pair_cols(b_q + h * LANES, b_q + h * LANES + HEAD_DIM) for h in range(B_HEADS)]
    cols += [_pair_cols(a_k, a_k + HEAD_DIM)]
    cols += [_pair_cols(b_k + h * LANES, b_k + h * LANES + HEAD_DIM) for h in range(B_HEADS)]
    cols += [np.arange(a_v, b_k), np.arange(b_v, EVEN_IN)]
    return np.concatenate(cols)


def _even_out_row_perm():
    grp = A_HEADS // A_KV_HEADS
    rows = []
    for j in range(grp):
        rows += [np.arange(j * HEAD_DIM, (j + 1) * HEAD_DIM),
                 np.arange((grp + j) * HEAD_DIM, (grp + j + 1) * HEAD_DIM)]
    rows.append(np.arange(MIX_HALF, D_MODEL))
    return np.concatenate(rows)


def _block_mean_matrix():
    seg = (np.arange(MXU_DIM) // LANES) * 2 + (np.arange(MXU_DIM) // HALF) % 2
    return jnp.asarray((seg[:, None] == seg[None, :]).astype(np.float32) / HEAD_DIM, dtype=BF16)


def _rope_tables(n_tokens):
    rows = n_tokens // GRID_W
    row = jnp.repeat(jnp.arange(rows, dtype=jnp.int32), GRID_W).astype(F32)
    col = jnp.tile(jnp.arange(GRID_W, dtype=jnp.int32), rows).astype(F32)
    inv = ROPE_THETA ** (-jnp.arange(ROPE_AXIS_PAIRS, dtype=F32) / ROPE_AXIS_PAIRS)
    ang = jnp.concatenate([row[:, None] * inv, col[:, None] * inv], axis=-1)
    c, s = jnp.cos(ang), jnp.sin(ang)
    one, zero = jnp.ones_like(c), jnp.zeros_like(c)
    even = (jnp.concatenate([c, c, c, c], axis=1), jnp.concatenate([-s, -s, s, s], axis=1))
    odd = (jnp.concatenate([one, c, one, c], axis=1), jnp.concatenate([zero, -s, zero, s], axis=1))
    return even, odd


def _pad_rows(a, rows):
    return jnp.concatenate([a, jnp.zeros((rows - a.shape[0],) + a.shape[1:], a.dtype)], axis=0)


def kernel(x, c, ctx, c_ctx, norm1_g, norm2_g, ada_w, ada_b, mix_w_out, ffn_w_in, ffn_w_out, ev_w_in, ev_qnorm_a, ev_knorm_a, ev_sink, ev_qnorm_b, ev_knorm_b, ev_lam_q1, ev_lam_k1, ev_lam_q2, ev_lam_k2, ev_subln, od_w_in, od_c_vnorm, od_c_ws, od_c_bs, od_qa_norm, od_kva_norm, od_w_uq, od_w_ukv, od_qnorm_nope, od_knorm_nope, od_qnorm_rope, od_knorm_rope):
    bsz, seq, dm = x.shape
    lctx = ctx.shape[1]
    assert (dm, DEPTH) == (D_MODEL, 2) and seq % 512 == 0 and lctx % 256 == 0
    ctx_row = bsz

    cv = _pad_rows(jnp.concatenate([c, c_ctx[None, :]], axis=0), 16)
    mods = _adaln(cv, ada_w, ada_b).reshape(DEPTH, 16, 1, 6 * dm)
    (cos_e, sin_e), (cos_o, sin_o) = _rope_tables(seq)
    m256 = _block_mean_matrix()
    lat_row = lambda b: b
    ctx_mod = lambda b: ctx_row

    w_in0 = ev_w_in[0][:, _even_col_perm()].astype(BF16)
    sa = HEAD_DIM ** -0.5 * LOG2E
    gains0 = _pad_rows(jnp.stack(
        [_pair_gain(ev_qnorm_a[0], ev_qnorm_a[0], sa)] * 4 + [_pair_gain(ev_qnorm_b[0], ev_qnorm_b[0], sa)] * 4
        + [_pair_gain(ev_knorm_a[0], ev_knorm_a[0])] + [_pair_gain(ev_knorm_b[0], ev_knorm_b[0])] * 4), 16)
    g1 = norm1_g[0][None, :]
    pl0 = _proj_even(x, mods[0], lat_row, g1, w_in0, m256, gains0, cos_e, sin_e, tm=512)
    pc0 = _proj_even(ctx, mods[0], ctx_mod, g1, w_in0, m256, gains0, None, None, tm=lctx)

    lamv = _pad_rows(jnp.stack([ev_lam_q1[0], ev_lam_k1[0], ev_lam_q2[0], ev_lam_k2[0]]), 8)
    subln = ev_subln[0][None, :]
    lam_init = 0.8 - 0.6 * math.exp(-0.3 * 0)
    sink = ev_sink[0].astype(F32)
    out_a = _attn_a(pl0, pc0, sink, local=True, tq=2 * A_WINDOW)
    out_ac = _attn_a(pc0, pc0, sink, local=False, tq=lctx)
    out_b = _attn_full(pl0, EV_QB, pl0, pc0, EV_KB, EV_VB, B_HEADS, 256, True, lamv, subln, lam_init,
                       name="attn_b_lat")
    out_bc = _attn_full(pc0, EV_QB, None, pc0, EV_KB, EV_VB, B_HEADS, lctx, True, lamv, subln, lam_init,
                        name="attn_b_ctx")

    wo0 = mix_w_out[0][_even_out_row_perm(), :].astype(BF16)
    wi0 = ffn_w_in[0].astype(BF16)
    wo20 = ffn_w_out[0].astype(BF16)
    g2 = norm2_g[0][None, :]
    x1 = _out_ffn(x, out_a, 0, out_b, mods[0], lat_row, g2, wo0, wi0, wo20, 512, "out_ffn0_lat")
    xc1 = _out_ffn(ctx, out_ac, 0, out_bc, mods[0], ctx_mod, g2, wo0, wi0, wo20, lctx, "out_ffn0_ctx")

    w1 = od_w_in[0]
    z32 = jnp.zeros((dm, HALF), F32)
    kv0 = ODD_C + D_Q_RANK
    pe0 = kv0 + D_KV_RANK
    kv_cols = jnp.concatenate([w1[:, kv0:pe0], z32, w1[:, pe0:pe0 + HALF], z32, w1[:, pe0 + HALF:]], axis=1)
    w_in1 = jnp.concatenate([w1[:, :kv0], kv_cols], axis=1).astype(BF16)
    w_in1_ctx = kv_cols.astype(BF16)
    wuq = od_w_uq[0][:, np.concatenate(
        [_pair_cols(h * LANES, h * LANES + D_NOPE) for h in range(D_HEADS)])].astype(BF16)
    per_head = D_NOPE + D_VDIM
    zcol = D_HEADS * per_head
    lo = np.arange(HALF)
    kcols = np.concatenate([np.concatenate([h * per_head + lo, np.full(HALF, zcol), h * per_head + HALF + lo,
                                            np.full(HALF, zcol)]) for h in range(D_HEADS)])
    vcols = np.concatenate([h * per_head + D_NOPE + np.arange(D_VDIM) for h in range(D_HEADS)])
    wukv = jnp.concatenate([od_w_ukv[0], jnp.zeros((D_KV_RANK, 1), F32)], axis=1)[
        :, np.concatenate([kcols, vcols])].astype(BF16)
    sd = (D_NOPE + D_ROPE) ** -0.5 * LOG2E
    qgain = _pad_rows(jnp.stack([_pair_gain(od_qnorm_nope[0], od_qnorm_rope[0], sd)] * D_HEADS), 8)
    kgain = _pad_rows(jnp.stack([_pair_gain(od_knorm_nope[0], od_knorm_rope[0])] * D_HEADS), 8)
    ws = od_c_ws[0].astype(BF16)
    bs = jnp.broadcast_to(od_c_bs[0][:, :, None], (C_GROUPS, C_CHUNK, LANES)).astype(F32)
    g1 = norm1_g[1][None, :]
    kvan = od_kva_norm[0][None, :]
    pl1 = _proj_odd(x1, mods[1], g1, w_in1, od_c_vnorm[0][None, :], ws, bs, od_qa_norm[0][None, :], kvan,
                    wuq, wukv, m256, qgain, kgain, cos_o, sin_o, tm=512)
    pc1 = _proj_odd_ctx(xc1, mods[1], ctx_mod, g1, w_in1_ctx, kvan, wukv, m256, kgain, tm=lctx)
    out_d = _attn_full(pl1, OD_Q, pl1, pc1, OD_K, OD_V, D_HEADS, 512, False, name="attn_d_lat")

    return _out_ffn(x1, pl1, 0, out_d, mods[1], lat_row, norm2_g[1][None, :], mix_w_out[1].astype(BF16),
                    ffn_w_in[1].astype(BF16), ffn_w_out[1].astype(BF16), 512, "out_ffn1_lat")
```

```python
import functools
import math

import numpy as np
import jax
import jax.numpy as jnp
from jax import lax
from jax.experimental import pallas as pl
from jax.experimental.pallas import tpu as pltpu

F32 = jnp.float32
BF16 = jnp.bfloat16

D_MODEL = 1024
DEPTH = 2
GRID_W = 64
HEAD_DIM = 64
HALF = HEAD_DIM // 2
ROPE_THETA = 10000.0
ROPE_AXIS_PAIRS = HEAD_DIM // 4
EPS = 1e-6
NEG = -1e30
MIX_HALF = D_MODEL // 2
A_HEADS = 8
A_KV_HEADS = 2
A_WINDOW = 128
B_HEADS = 4
B_VDIM = 128
C_WIDTH = MIX_HALF
C_GROUPS = 4
C_CHUNK = 128
D_NOPE = 64
D_ROPE = 64
D_VDIM = 128
D_HEADS = 4
D_Q_RANK = 256
D_KV_RANK = 256
FFN_HIDDEN = 2816
EVEN_IN = 2304
ODD_C = 2 * C_WIDTH

LANES = 128
MXU_DIM = 256
VMEM_LIMIT = 56 * 1024 * 1024
ATTN_KEY_CHUNK = 256
LOG2E = math.log2(math.e)

EV_QA, EV_QB, EV_KA, EV_KB, EV_VA, EV_VB = 0, 4, 8, 9, 13, 14
EV_NORM_TILES = 13
OD_C, OD_Q, OD_K, OD_V = 0, 4, 8, 12
OD_OUT = 2048
OD_IN_PAD = 1664


def _dot(a, b):
    return jnp.dot(a, b, preferred_element_type=F32)


def _dot_nt(a, b):
    return lax.dot_general(a, b, (((1,), (1,)), ((), ())), preferred_element_type=F32)


def _lane_is_a(shape):
    lane = lax.broadcasted_iota(jnp.int32, shape, len(shape) - 1)
    return (lane // HALF) % 2 == 0


def _modulated_norm(x, g, shift, scale):
    ms = jnp.mean(x * x, axis=-1, keepdims=True)
    return (x * lax.rsqrt(ms + EPS) * g) * (1.0 + scale) + shift


def _norm_rope_tiles(t, m_ref, gains_ref, n_tiles, cos, sin):
    out = []
    j = 0
    while j < n_tiles:
        width = MXU_DIM if j + 1 < n_tiles else LANES
        tp = t[:, j * LANES:j * LANES + width]
        msq = _dot((tp * tp).astype(BF16), m_ref[:width, :width])
        for q in range(width // LANES):
            tt = tp[:, q * LANES:(q + 1) * LANES]
            r = lax.rsqrt(msq[:, q * LANES:(q + 1) * LANES] + EPS)
            tn = tt * r * gains_ref[j + q:j + q + 1, :]
            if cos is not None:
                tn = tn * cos + pltpu.roll(tn, 2 * HALF, 1) * sin
            out.append(tn.astype(BF16))
        j += width // LANES
    return out


def _adaln_kernel(cv_ref, w_ref, b_ref, o_ref):
    a = cv_ref[...]
    a = a * jax.nn.sigmoid(a)
    w = w_ref[...]
    a_hi = a.astype(BF16)
    a_lo = (a - a_hi.astype(F32)).astype(BF16)
    w_hi = w.astype(BF16)
    w_lo = (w - w_hi.astype(F32)).astype(BF16)
    o_ref[...] = _dot(a_hi, w_hi) + _dot(a_hi, w_lo) + _dot(a_lo, w_hi) + b_ref[...]


def _adaln(cv, ada_w, ada_b):
    tn = 768
    n6 = 6 * D_MODEL
    rows = cv.shape[0]
    return pl.pallas_call(
        _adaln_kernel,
        grid=(DEPTH, n6 // tn),
        in_specs=[
            pl.BlockSpec((rows, D_MODEL), lambda l, j: (0, 0)),
            pl.BlockSpec((None, D_MODEL, tn), lambda l, j: (l, 0, j)),
            pl.BlockSpec((None, 1, tn), lambda l, j: (l, 0, j)),
        ],
        out_specs=pl.BlockSpec((None, rows, tn), lambda l, j: (l, 0, j)),
        out_shape=jax.ShapeDtypeStruct((DEPTH, rows, n6), F32),
        compiler_params=pltpu.CompilerParams(
            dimension_semantics=("arbitrary", "arbitrary"), vmem_limit_bytes=VMEM_LIMIT),
        name="adaln",
    )(cv, ada_w, ada_b.reshape(DEPTH, 1, n6))


def _proj_even_kernel(*refs, rope):
    if rope:
        x_ref, mod_ref, g_ref, w_ref, m_ref, gains_ref, cos_ref, sin_ref, o_ref = refs
        cos, sin = cos_ref[...], sin_ref[...]
    else:
        x_ref, mod_ref, g_ref, w_ref, m_ref, gains_ref, o_ref = refs
        cos = sin = None
    h = _modulated_norm(x_ref[...], g_ref[...], mod_ref[:, 0:D_MODEL], mod_ref[:, D_MODEL:2 * D_MODEL])
    t = _dot(h.astype(BF16), w_ref[...])
    tiles = _norm_rope_tiles(t, m_ref, gains_ref, EV_NORM_TILES, cos, sin)
    for j, tile in enumerate(tiles):
        o_ref[:, j * LANES:(j + 1) * LANES] = tile
    o_ref[:, EV_NORM_TILES * LANES:] = t[:, EV_NORM_TILES * LANES:].astype(BF16)


def _const_spec(shape):
    nd = len(shape)
    return pl.BlockSpec(shape, lambda *_: (0,) * nd, pipeline_mode=pl.Buffered(1))


def _proj_even(xs, mods, mod_row, g, w, m256, gains, cos, sin, tm):
    bsz, n, _ = xs.shape
    rope = cos is not None
    in_specs = [
        pl.BlockSpec((None, tm, D_MODEL), lambda b, i: (b, i, 0)),
        pl.BlockSpec((None, 1, 6 * D_MODEL), lambda b, i: (mod_row(b), 0, 0)),
        _const_spec((1, D_MODEL)),
        _const_spec((D_MODEL, EVEN_IN)),
        _const_spec((MXU_DIM, MXU_DIM)),
        _const_spec(gains.shape),
    ]
    args = [xs, mods, g, w, m256, gains]
    if rope:
        in_specs += [pl.BlockSpec((tm, LANES), lambda b, i: (i, 0))] * 2
        args += [cos, sin]
    return pl.pallas_call(
        functools.partial(_proj_even_kernel, rope=rope),
        grid=(bsz, n // tm),
        in_specs=in_specs,
        out_specs=pl.BlockSpec((None, tm, EVEN_IN), lambda b, i: (b, i, 0)),
        out_shape=jax.ShapeDtypeStruct((bsz, n, EVEN_IN), BF16),
        compiler_params=pltpu.CompilerParams(
            dimension_semantics=("parallel", "parallel"), vmem_limit_bytes=VMEM_LIMIT),
        name="proj_even_lat" if rope else "proj_even_ctx",
    )(*args)


def _sub_lane_tiles(s, m):
    return jnp.concatenate([s[:, t * LANES:(t + 1) * LANES] - m for t in range(s.shape[1] // LANES)], axis=1)


def _attn_a_kernel(*refs, local, tq, n_units, nq):
    grp = A_HEADS // A_KV_HEADS
    refs = list(refs)
    m_scr = refs.pop()
    s_scr = refs.pop()
    o_ref = refs.pop()
    sink_ref, q_ref = refs[:2]
    j = pl.program_id(0)
    if local:
        kp_ref, kc_ref, kn_ref, vp_ref, vc_ref, vn_ref, band_ref, kx_ref, vx_ref = refs[2:]
        i = jnp.minimum(j, n_units - 1) % nq
        neg_first = jnp.where(i == 0, NEG, 0.0)
        neg_last = jnp.where(i == nq - 1, NEG, 0.0)
        w = A_WINDOW
        srcs = [(kp_ref, vp_ref, band_ref[:, :w] + neg_first), (kc_ref, vc_ref, band_ref[:, w:w + tq]),
                (kn_ref, vn_ref, band_ref[:, w + tq:] + neg_last), (kx_ref, vx_ref, None)]
    else:
        kx_ref, vx_ref = refs[2:]
        srcs = [(kx_ref, vx_ref, None)]

    @pl.when(j == 0)
    def _():
        s_scr[...] = jnp.zeros(s_scr.shape, F32)
        m_scr[...] = jnp.zeros(m_scr.shape, F32)

    is_a = _lane_is_a((1, LANES))
    q = q_ref[...]
    zero = jnp.zeros((tq, LANES), BF16)
    qs = jnp.concatenate(
        [jnp.where(is_a if kvh == 0 else jnp.logical_not(is_a), q[:, t * LANES:(t + 1) * LANES], zero)
         for kvh in range(A_KV_HEADS) for t in range(grp)], axis=0)
    sink = jnp.concatenate([jnp.full((tq, LANES), sink_ref[h] * LOG2E, F32) for h in range(A_HEADS)], axis=0)
    m_prev = m_scr[...]
    acc = l_part = m_part = None
    col = 0
    for k_ref, v_ref, bias in srcs:
        sz = k_ref.shape[0]
        e = jnp.exp2(_sub_lane_tiles(s_scr[:, col:col + sz], m_prev))
        lc = _lane_fold(e, jnp.add)
        l_part = lc if l_part is None else l_part + lc
        pv = _dot(e.astype(BF16), v_ref[...])
        acc = pv if acc is None else acc + pv
        s_new = _dot_nt(qs, k_ref[...])
        if bias is not None:
            s_new = (s_new.reshape(A_HEADS, tq, sz) + bias[None]).reshape(A_HEADS * tq, sz)
        s_scr[:, col:col + sz] = s_new
        mc = _lane_fold(s_new, jnp.maximum)
        m_part = mc if m_part is None else jnp.maximum(m_part, mc)
        col += sz
    m_new = jnp.broadcast_to(jnp.max(m_part, axis=-1, keepdims=True), m_scr.shape)
    m_scr[...] = jnp.maximum(m_new, sink)
    l = jnp.broadcast_to(jnp.sum(l_part, axis=-1, keepdims=True), m_scr.shape) + jnp.exp2(sink - m_prev)
    o = (acc / l).astype(BF16)
    for h in range(A_HEADS):
        o_ref[:, h * LANES:(h + 1) * LANES] = o[h * tq:(h + 1) * tq]


def _attn_a(qsrc, pc, sink, local, tq):
    bsz, n, _ = qsrc.shape
    lctx = pc.shape[1]
    nq = n // tq
    r = tq // A_WINDOW
    nblk = n // A_WINDOW
    qw = MIX_HALF
    n_units = bsz * nq

    def unit(u):
        return u // nq, u % nq

    def cur(j):
        return unit(jnp.minimum(j, n_units - 1))

    def prev(j):
        return unit(jnp.maximum(j - 1, 0))

    def before(which, col):
        return lambda j: (which(j)[0], jnp.maximum(which(j)[1] * r - 1, 0), col)

    def at(which, col):
        return lambda j: (which(j)[0], which(j)[1], col)

    def after(which, col):
        return lambda j: (which(j)[0], jnp.minimum((which(j)[1] + 1) * r, nblk - 1), col)

    in_specs = [pl.BlockSpec(memory_space=pltpu.SMEM), pl.BlockSpec((None, tq, qw), at(cur, 0))]
    args = [sink, qsrc]
    n_keys = lctx
    if local:
        for which, col in ((cur, EV_KA), (prev, EV_VA)):
            in_specs += [pl.BlockSpec((None, A_WINDOW, LANES), before(which, col)),
                         pl.BlockSpec((None, tq, LANES), at(which, col)),
                         pl.BlockSpec((None, A_WINDOW, LANES), after(which, col))]
            args += [qsrc, qsrc, qsrc]
        off = np.arange(tq + 2 * A_WINDOW)[None, :] - A_WINDOW
        band = np.where(np.abs(np.arange(tq)[:, None] - off) <= A_WINDOW, 0.0, NEG).astype(np.float32)
        in_specs.append(_const_spec(band.shape))
        args.append(jnp.asarray(band))
        n_keys += tq + 2 * A_WINDOW
    in_specs += [pl.BlockSpec((None, lctx, LANES), lambda j: (cur(j)[0], 0, EV_KA)),
                 pl.BlockSpec((None, lctx, LANES), lambda j: (prev(j)[0], 0, EV_VA))]
    args += [pc, pc]
    rows = A_HEADS * tq
    return pl.pallas_call(
        functools.partial(_attn_a_kernel, local=local, tq=tq, n_units=n_units, nq=nq),
        grid=(n_units + 1,),
        in_specs=in_specs,
        out_specs=pl.BlockSpec((None, tq, A_HEADS * LANES), at(prev, 0)),
        out_shape=jax.ShapeDtypeStruct((bsz, n, A_HEADS * LANES), BF16),
        scratch_shapes=[pltpu.VMEM((rows, n_keys), F32), pltpu.VMEM((rows, LANES), F32)],
        compiler_params=pltpu.CompilerParams(
            dimension_semantics=("arbitrary",), vmem_limit_bytes=VMEM_LIMIT),
        name="attn_a_lat" if local else "attn_a_ctx",
    )(*args)


def _lane_fold(x, op):
    out = x[:, :LANES]
    for t in range(1, x.shape[1] // LANES):
        out = op(out, x[:, t * LANES:(t + 1) * LANES])
    return out


def _attn_full_kernel(*refs, diff, has_lat, lam_init, tq, ck):
    refs = list(refs)
    m_scr = refs.pop()
    s_scr = refs.pop()
    o_ref = refs.pop()
    if diff:
        lamv_ref = refs.pop(0)
        subln_ref = refs.pop()
    q_ref = refs.pop(0)
    srcs = [(refs[0], refs[1]), (refs[2], refs[3])] if has_lat else [(refs[0], refs[1])]

    @pl.when(pl.program_id(0) == 0)
    def _():
        s_scr[...] = jnp.zeros(s_scr.shape, F32)
        m_scr[...] = jnp.zeros(m_scr.shape, F32)

    q = q_ref[...]
    if diff:
        is_a = _lane_is_a((1, LANES))
        zero = jnp.zeros_like(q)
        qs = jnp.concatenate([jnp.where(is_a, q, zero), jnp.where(is_a, zero, q)], axis=0)
    else:
        qs = q
    m_prev = m_scr[...]
    acc = l_part = m_part = None
    col = 0
    for k_ref, v_ref in srcs:
        nk = k_ref.shape[0]
        for st in range(0, nk, ck):
            sz = min(ck, nk - st)
            s_old = s_scr[:, col:col + sz]
            e = jnp.exp2(jnp.concatenate(
                [s_old[:, t * LANES:(t + 1) * LANES] - m_prev for t in range(sz // LANES)], axis=1))
            lc = _lane_fold(e, jnp.add)
            l_part = lc if l_part is None else l_part + lc
            pv = _dot(e.astype(BF16), v_ref[st:st + sz, :])
            acc = pv if acc is None else acc + pv
            s_new = _dot_nt(qs, k_ref[st:st + sz, :])
            s_scr[:, col:col + sz] = s_new
            mc = _lane_fold(s_new, jnp.maximum)
            m_part = mc if m_part is None else jnp.maximum(m_part, mc)
            col += sz
    m_scr[...] = jnp.broadcast_to(jnp.max(m_part, axis=-1, keepdims=True), m_scr.shape)
    o = acc / jnp.sum(l_part, axis=-1, keepdims=True)
    if diff:
        lv = lamv_ref[...]
        lam = (jnp.exp(jnp.sum(lv[0:1] * lv[1:2], axis=-1, keepdims=True))
               - jnp.exp(jnp.sum(lv[2:3] * lv[3:4], axis=-1, keepdims=True)) + lam_init)
        o = o[:tq] - lam * o[tq:]
        ms = jnp.mean(o * o, axis=-1, keepdims=True)
        o = (o * lax.rsqrt(ms + EPS) * subln_ref[...]) * (1.0 - lam_init)
    o_ref[...] = o.astype(BF16)


def _attn_full(qsrc, q_col, lat, ctx, k_col, v_col, nheads, tq, diff, lamv=None, subln=None, lam_init=0.0,
               ctx_cols=None, name="attn_full"):
    bsz, n, _ = qsrc.shape
    nq = n // tq
    n_tiles = bsz * nheads * nq

    def tile_of(j):
        return j // (nheads * nq), (j // nq) % nheads, j % nq

    def cur(j):
        return tile_of(jnp.minimum(j, n_tiles - 1))

    def prev(j):
        return tile_of(jnp.maximum(j - 1, 0))

    def q_map(j):
        b, h, i = cur(j)
        return b, i, q_col + h

    def k_map(j, col):
        b, h, _ = cur(j)
        return b, 0, col + h

    def v_map(j, col):
        b, h, _ = prev(j)
        return b, 0, col + h

    def o_map(j):
        b, h, i = prev(j)
        return b, i, h

    in_specs, args = [], []
    if diff:
        in_specs.append(_const_spec(lamv.shape))
        args.append(lamv)
    in_specs.append(pl.BlockSpec((None, tq, LANES), q_map))
    args.append(qsrc)
    ckc, cvc = (k_col, v_col) if ctx_cols is None else ctx_cols
    for src, kc, vc in ([(lat, k_col, v_col)] if lat is not None else []) + [(ctx, ckc, cvc)]:
        nk = src.shape[1]
        in_specs += [pl.BlockSpec((None, nk, LANES), functools.partial(k_map, col=kc)),
                     pl.BlockSpec((None, nk, LANES), functools.partial(v_map, col=vc))]
        args += [src, src]
    if diff:
        in_specs.append(_const_spec(subln.shape))
        args.append(subln)
    rows = 2 * tq if diff else tq
    n_keys = ctx.shape[1] + (lat.shape[1] if lat is not None else 0)
    return pl.pallas_call(
        functools.partial(_attn_full_kernel, diff=diff, has_lat=lat is not None, lam_init=lam_init, tq=tq,
                          ck=ATTN_KEY_CHUNK),
        grid=(n_tiles + 1,),
        in_specs=in_specs,
        out_specs=pl.BlockSpec((None, tq, LANES), o_map),
        out_shape=jax.ShapeDtypeStruct((bsz, n, nheads * LANES), BF16),
        scratch_shapes=[pltpu.VMEM((rows, n_keys), F32), pltpu.VMEM((rows, LANES), F32)],
        compiler_params=pltpu.CompilerParams(
            dimension_semantics=("arbitrary",), vmem_limit_bytes=VMEM_LIMIT),
        name=name,
    )(*args)


FFN_CHUNK = MXU_DIM


def _out_ffn_kernel(x_ref, ma_ref, mb_ref, mod_ref, g_ref, woa_ref, wob_ref, wi_ref, wo2_ref, o_ref):
    dm = D_MODEL
    mix = _dot(ma_ref[...], woa_ref[...]) + _dot(mb_ref[...], wob_ref[...])
    x1 = x_ref[...] + mod_ref[:, 2 * dm:3 * dm] * mix
    h2 = _modulated_norm(x1, g_ref[...], mod_ref[:, 3 * dm:4 * dm], mod_ref[:, 4 * dm:5 * dm]).astype(BF16)
    acc = jnp.zeros(x1.shape, F32)
    for c in range(FFN_HIDDEN // FFN_CHUNK):
        lo = c * FFN_CHUNK
        gate = _dot(h2, wi_ref[:, lo:lo + FFN_CHUNK])
        up = _dot(h2, wi_ref[:, FFN_HIDDEN + lo:FFN_HIDDEN + lo + FFN_CHUNK])
        act = (gate * jax.nn.sigmoid(gate) * up).astype(BF16)
        acc = acc + _dot(act, wo2_ref[lo:lo + FFN_CHUNK, :])
    o_ref[...] = x1 + mod_ref[:, 5 * dm:6 * dm] * acc


def _out_ffn(xs, ma, mb, mods, mod_row, g, woa, wob, wi, wo2, tm, name):
    bsz, n, _ = xs.shape
    return pl.pallas_call(
        _out_ffn_kernel,
        grid=(bsz, n // tm),
        in_specs=[
            pl.BlockSpec((None, tm, D_MODEL), lambda b, i: (b, i, 0)),
            pl.BlockSpec((None, tm, woa.shape[0]), lambda b, i: (b, i, 0)),
            pl.BlockSpec((None, tm, wob.shape[0]), lambda b, i: (b, i, 0)),
            pl.BlockSpec((None, 1, 6 * D_MODEL), lambda b, i: (mod_row(b), 0, 0)),
            _const_spec((1, D_MODEL)),
            _const_spec(woa.shape),
            _const_spec(wob.shape),
            _const_spec((D_MODEL, 2 * FFN_HIDDEN)),
            _const_spec((FFN_HIDDEN, D_MODEL)),
        ],
        out_specs=pl.BlockSpec((None, tm, D_MODEL), lambda b, i: (b, i, 0)),
        out_shape=jax.ShapeDtypeStruct(xs.shape, F32),
        compiler_params=pltpu.CompilerParams(
            dimension_semantics=("parallel", "parallel"), vmem_limit_bytes=VMEM_LIMIT),
        name=name,
    )(xs, ma, mb, mods, g, woa, wob, wi, wo2)


def _mla_keys_values(ckv, kpe_tile, kvan_ref, wukv_ref, m_ref, kgain_ref, cos, sin):
    ms = jnp.mean(ckv * ckv, axis=-1, keepdims=True)
    ckv_n = (ckv * lax.rsqrt(ms + EPS) * kvan_ref[...]).astype(BF16)
    kv = _dot(ckv_n, wukv_ref[...])
    nk = D_HEADS * LANES
    kraw = jnp.concatenate(
        [kv[:, h * LANES:(h + 1) * LANES] + kpe_tile for h in range(D_HEADS)], axis=1)
    ktiles = _norm_rope_tiles(kraw, m_ref, kgain_ref, D_HEADS, cos, sin)
    return ktiles, kv[:, nk:].astype(BF16)


def _proj_odd_kernel(x_ref, mod_ref, g_ref, w_ref, vn_ref, ws_ref, bs_ref, qan_ref, kvan_ref, wuq_ref,
                     wukv_ref, m_ref, qgain_ref, kgain_ref, cos_ref, sin_ref, o_ref, *, tm):
    h = _modulated_norm(x_ref[...], g_ref[...], mod_ref[:, 0:D_MODEL], mod_ref[:, D_MODEL:2 * D_MODEL])
    t = _dot(h.astype(BF16), w_ref[...])
    cos, sin = cos_ref[...], sin_ref[...]
    uv = t[:, :ODD_C]
    uv = 0.5 * uv * (1.0 + lax.erf(uv * math.sqrt(0.5)))
    u = uv[:, :C_WIDTH]
    v = uv[:, C_WIDTH:]
    ms = jnp.mean(v * v, axis=-1, keepdims=True)
    v = (v * lax.rsqrt(ms + EPS) * vn_ref[...]).astype(BF16)
    for c in range(tm // C_CHUNK):
        r0 = c * C_CHUNK
        for gidx in range(C_GROUPS):
            c0 = gidx * LANES
            vs = _dot(ws_ref[gidx], v[r0:r0 + C_CHUNK, c0:c0 + LANES]) + bs_ref[gidx]
            o_ref[r0:r0 + C_CHUNK, c0:c0 + LANES] = (u[r0:r0 + C_CHUNK, c0:c0 + LANES] * vs).astype(BF16)
    cq = t[:, ODD_C:ODD_C + D_Q_RANK]
    ms = jnp.mean(cq * cq, axis=-1, keepdims=True)
    cq_n = (cq * lax.rsqrt(ms + EPS) * qan_ref[...]).astype(BF16)
    qtiles = _norm_rope_tiles(_dot(cq_n, wuq_ref[...]), m_ref, qgain_ref, D_HEADS, cos, sin)
    for hh, tile in enumerate(qtiles):
        o_ref[:, (OD_Q + hh) * LANES:(OD_Q + hh + 1) * LANES] = tile
    k0 = ODD_C + D_Q_RANK
    ktiles, vals = _mla_keys_values(t[:, k0:k0 + D_KV_RANK], t[:, k0 + D_KV_RANK:], kvan_ref, wukv_ref,
                                    m_ref, kgain_ref, cos, sin)
    for hh, tile in enumerate(ktiles):
        o_ref[:, (OD_K + hh) * LANES:(OD_K + hh + 1) * LANES] = tile
    o_ref[:, OD_V * LANES:] = vals


def _proj_odd_ctx_kernel(x_ref, mod_ref, g_ref, w_ref, kvan_ref, wukv_ref, m_ref, kgain_ref, o_ref):
    h = _modulated_norm(x_ref[...], g_ref[...], mod_ref[:, 0:D_MODEL], mod_ref[:, D_MODEL:2 * D_MODEL])
    t = _dot(h.astype(BF16), w_ref[...])
    ktiles, vals = _mla_keys_values(t[:, :D_KV_RANK], t[:, D_KV_RANK:], kvan_ref, wukv_ref, m_ref, kgain_ref,
                                    None, None)
    for hh, tile in enumerate(ktiles):
        o_ref[:, hh * LANES:(hh + 1) * LANES] = tile
    o_ref[:, D_HEADS * LANES:] = vals


def _proj_odd(xs, mods, g, w, vn, ws, bs, qan, kvan, wuq, wukv, m256, qgain, kgain, cos, sin, tm):
    bsz, n, _ = xs.shape
    consts = [g, w, vn, ws, bs, qan, kvan, wuq, wukv, m256, qgain, kgain]
    return pl.pallas_call(
        functools.partial(_proj_odd_kernel, tm=tm),
        grid=(bsz, n // tm),
        in_specs=[pl.BlockSpec((None, tm, D_MODEL), lambda b, i: (b, i, 0)),
                  pl.BlockSpec((None, 1, 6 * D_MODEL), lambda b, i: (b, 0, 0))]
        + [_const_spec(a.shape) for a in consts]
        + [pl.BlockSpec((tm, LANES), lambda b, i: (i, 0))] * 2,
        out_specs=pl.BlockSpec((None, tm, OD_OUT), lambda b, i: (b, i, 0)),
        out_shape=jax.ShapeDtypeStruct((bsz, n, OD_OUT), BF16),
        compiler_params=pltpu.CompilerParams(
            dimension_semantics=("parallel", "parallel"), vmem_limit_bytes=VMEM_LIMIT),
        name="proj_odd_lat",
    )(xs, mods, *consts, cos, sin)


def _proj_odd_ctx(xs, mods, mod_row, g, w, kvan, wukv, m256, kgain, tm):
    bsz, n, _ = xs.shape
    consts = [g, w, kvan, wukv, m256, kgain]
    width = 2 * D_HEADS * LANES
    return pl.pallas_call(
        _proj_odd_ctx_kernel,
        grid=(bsz, n // tm),
        in_specs=[pl.BlockSpec((None, tm, D_MODEL), lambda b, i: (b, i, 0)),
                  pl.BlockSpec((None, 1, 6 * D_MODEL), lambda b, i: (mod_row(b), 0, 0))]
        + [_const_spec(a.shape) for a in consts],
        out_specs=pl.BlockSpec((None, tm, width), lambda b, i: (b, i, 0)),
        out_shape=jax.ShapeDtypeStruct((bsz, n, width), BF16),
        compiler_params=pltpu.CompilerParams(
            dimension_semantics=("parallel", "parallel"), vmem_limit_bytes=VMEM_LIMIT),
        name="proj_odd_ctx",
    )(xs, mods, *consts)


def _pair_cols(ca, cb):
    lo = np.arange(HALF)
    return np.concatenate([ca + lo, cb + lo, ca + HALF + lo, cb + HALF + lo])


def _pair_gain(ga, gb, scale=1.0):
    return jnp.concatenate([ga[:HALF], gb[:HALF], ga[HALF:], gb[HALF:]]).astype(F32) * scale


def _even_col_perm():
    a_q, b_q = 0, A_HEADS * HEAD_DIM
    a_k = 2 * MIX_HALF
    a_v = a_k + A_KV_HEADS * HEAD_DIM
    b_k = a_v + A_KV_HEADS * HEAD_DIM
    b_v = b_k + B_HEADS * 2 * HEAD_DIM
    grp = A_HEADS // A_KV_HEADS
    cols = [_pair_cols(a_q + j * HEAD_DIM, a_q + (grp + j) * HEAD_DIM) for j in range(grp)]
    cols += [_pair_cols(b_q + h * LANES, b_q + h * LANES + HEAD_DIM) for h in range(B_HEADS)]
    cols += [_pair_cols(a_k, a_k + HEAD_DIM)]
    cols += [_pair_cols(b_k + h * LANES, b_k + h * LANES + HEAD_DIM) for h in range(B_HEADS)]
    cols += [np.arange(a_v, b_k), np.arange(b_v, EVEN_IN)]
    return np.concatenate(cols)


def _mixer_a_out_rows(zero_row):
    grp = A_HEADS // A_KV_HEADS
    rows = []
    for kvh in range(A_KV_HEADS):
        for t in range(grp):
            head = np.arange(HEAD_DIM) + (kvh * grp + t) * HEAD_DIM
            pad = np.full(HEAD_DIM, zero_row)
            rows += [head, pad] if kvh == 0 else [pad, head]
    return np.concatenate(rows)


def _block_mean_matrix():
    seg = (np.arange(MXU_DIM) // LANES) * 2 + (np.arange(MXU_DIM) // HALF) % 2
    return jnp.asarray((seg[:, None] == seg[None, :]).astype(np.float32) / HEAD_DIM, dtype=BF16)


def _rope_tables(n_tokens):
    rows = n_tokens // GRID_W
    row = jnp.repeat(jnp.arange(rows, dtype=jnp.int32), GRID_W).astype(F32)
    col = jnp.tile(jnp.arange(GRID_W, dtype=jnp.int32), rows).astype(F32)
    inv = ROPE_THETA ** (-jnp.arange(ROPE_AXIS_PAIRS, dtype=F32) / ROPE_AXIS_PAIRS)
    ang = jnp.concatenate([row[:, None] * inv, col[:, None] * inv], axis=-1)
    c, s = jnp.cos(ang), jnp.sin(ang)
    one, zero = jnp.ones_like(c), jnp.zeros_like(c)
    even = (jnp.concatenate([c, c, c, c], axis=1), jnp.concatenate([-s, -s, s, s], axis=1))
    odd = (jnp.concatenate([one, c, one, c], axis=1), jnp.concatenate([zero, -s, zero, s], axis=1))
    return even, odd


def _pad_rows(a, rows):
    return jnp.concatenate([a, jnp.zeros((rows - a.shape[0],) + a.shape[1:], a.dtype)], axis=0)


def kernel(x, c, ctx, c_ctx, norm1_g, norm2_g, ada_w, ada_b, mix_w_out, ffn_w_in, ffn_w_out, ev_w_in, ev_qnorm_a, ev_knorm_a, ev_sink, ev_qnorm_b, ev_knorm_b, ev_lam_q1, ev_lam_k1, ev_lam_q2, ev_lam_k2, ev_subln, od_w_in, od_c_vnorm, od_c_ws, od_c_bs, od_qa_norm, od_kva_norm, od_w_uq, od_w_ukv, od_qnorm_nope, od_knorm_nope, od_qnorm_rope, od_knorm_rope):
    bsz, seq, dm = x.shape
    lctx = ctx.shape[1]
    assert (dm, DEPTH) == (D_MODEL, 2) and seq % 512 == 0 and lctx % 256 == 0
    ctx_row = bsz

    cv = _pad_rows(jnp.concatenate([c, c_ctx[None, :]], axis=0), 16)
    mods = _adaln(cv, ada_w, ada_b).reshape(DEPTH, 16, 1, 6 * dm)
    (cos_e, sin_e), (cos_o, sin_o) = _rope_tables(seq)
    m256 = _block_mean_matrix()
    lat_row = lambda b: b
    ctx_mod = lambda b: ctx_row

    w_in0 = ev_w_in[0][:, _even_col_perm()].astype(BF16)
    sa = HEAD_DIM ** -0.5 * LOG2E
    gains0 = _pad_rows(jnp.stack(
        [_pair_gain(ev_qnorm_a[0], ev_qnorm_a[0], sa)] * 4 + [_pair_gain(ev_qnorm_b[0], ev_qnorm_b[0], sa)] * 4
        + [_pair_gain(ev_knorm_a[0], ev_knorm_a[0])] + [_pair_gain(ev_knorm_b[0], ev_knorm_b[0])] * 4), 16)
    g1 = norm1_g[0][None, :]
    pl0 = _proj_even(x, mods[0], lat_row, g1, w_in0, m256, gains0, cos_e, sin_e, tm=512)
    pc0 = _proj_even(ctx, mods[0], ctx_mod, g1, w_in0, m256, gains0, None, None, tm=lctx)

    lamv = _pad_rows(jnp.stack([ev_lam_q1[0], ev_lam_k1[0], ev_lam_q2[0], ev_lam_k2[0]]), 8)
    subln = ev_subln[0][None, :]
    lam_init = 0.8 - 0.6 * math.exp(-0.3 * 0)
    sink = ev_sink[0].astype(F32)
    out_a = _attn_a(pl0, pc0, sink, local=True, tq=2 * A_WINDOW)
    out_ac = _attn_a(pc0, pc0, sink, local=False, tq=lctx)
    out_b = _attn_full(pl0, EV_QB, pl0, pc0, EV_KB, EV_VB, B_HEADS, 512, True, lamv, subln, lam_init,
                       name="attn_b_lat")
    out_bc = _attn_full(pc0, EV_QB, None, pc0, EV_KB, EV_VB, B_HEADS, lctx, True, lamv, subln, lam_init,
                        name="attn_b_ctx")

    wo0 = jnp.concatenate([mix_w_out[0], jnp.zeros((1, dm), F32)], axis=0)
    woa0 = wo0[_mixer_a_out_rows(dm), :].astype(BF16)
    wob0 = mix_w_out[0][MIX_HALF:, :].astype(BF16)
    wi0 = ffn_w_in[0].astype(BF16)
    wo20 = ffn_w_out[0].astype(BF16)
    g2 = norm2_g[0][None, :]
    x1 = _out_ffn(x, out_a, out_b, mods[0], lat_row, g2, woa0, wob0, wi0, wo20, 512, "out_ffn0_lat")
    xc1 = _out_ffn(ctx, out_ac, out_bc, mods[0], ctx_mod, g2, woa0, wob0, wi0, wo20, lctx, "out_ffn0_ctx")

    w1 = od_w_in[0]
    z32 = jnp.zeros((dm, HALF), F32)
    kv0 = ODD_C + D_Q_RANK
    pe0 = kv0 + D_KV_RANK
    kv_cols = jnp.concatenate([w1[:, kv0:pe0], z32, w1[:, pe0:pe0 + HALF], z32, w1[:, pe0 + HALF:]], axis=1)
    w_in1 = jnp.concatenate([w1[:, :kv0], kv_cols], axis=1).astype(BF16)
    w_in1_ctx = kv_cols.astype(BF16)
    wuq = od_w_uq[0][:, np.concatenate(
        [_pair_cols(h * LANES, h * LANES + D_NOPE) for h in range(D_HEADS)])].astype(BF16)
    per_head = D_NOPE + D_VDIM
    zcol = D_HEADS * per_head
    lo = np.arange(HALF)
    kcols = np.concatenate([np.concatenate([h * per_head + lo, np.full(HALF, zcol), h * per_head + HALF + lo,
                                            np.full(HALF, zcol)]) for h in range(D_HEADS)])
    vcols = np.concatenate([h * per_head + D_NOPE + np.arange(D_VDIM) for h in range(D_HEADS)])
    wukv = jnp.concatenate([od_w_ukv[0], jnp.zeros((D_KV_RANK, 1), F32)], axis=1)[
        :, np.concatenate([kcols, vcols])].astype(BF16)
    sd = (D_NOPE + D_ROPE) ** -0.5 * LOG2E
    qgain = _pad_rows(jnp.stack([_pair_gain(od_qnorm_nope[0], od_qnorm_rope[0], sd)] * D_HEADS), 8)
    kgain = _pad_rows(jnp.stack([_pair_gain(od_knorm_nope[0], od_knorm_rope[0])] * D_HEADS), 8)
    ws = od_c_ws[0].astype(BF16)
    bs = jnp.broadcast_to(od_c_bs[0][:, :, None], (C_GROUPS, C_CHUNK, LANES)).astype(F32)
    g1 = norm1_g[1][None, :]
    kvan = od_kva_norm[0][None, :]
    pl1 = _proj_odd(x1, mods[1], g1, w_in1, od_c_vnorm[0][None, :], ws, bs, od_qa_norm[0][None, :], kvan,
                    wuq, wukv, m256, qgain, kgain, cos_o, sin_o, tm=512)
    pc1 = _proj_odd_ctx(xc1, mods[1], ctx_mod, g1, w_in1_ctx, kvan, wukv, m256, kgain, tm=lctx)
    out_d = _attn_full(pl1, OD_Q, pl1, pc1, OD_K, OD_V, D_HEADS, 1024, False, ctx_cols=(0, D_HEADS),
                       name="attn_d_lat")

    wo1 = mix_w_out[1].astype(BF16)
    return _out_ffn(x1, pl1, out_d, mods[1], lat_row, norm2_g[1][None, :], wo1[:MIX_HALF], wo1[MIX_HALF:],
                    ffn_w_in[1].astype(BF16), ffn_w_out[1].astype(BF16), 512, "out_ffn1_lat")
```

```python
import functools
import math

import numpy as np
import jax
import jax.numpy as jnp
from jax import lax
from jax.experimental import pallas as pl
from jax.experimental.pallas import tpu as pltpu

F32 = jnp.float32
BF16 = jnp.bfloat16

D_MODEL = 1024
DEPTH = 2
GRID_W = 64
HEAD_DIM = 64
HALF = HEAD_DIM // 2
ROPE_THETA = 10000.0
ROPE_AXIS_PAIRS = HEAD_DIM // 4
EPS = 1e-6
NEG = -1e30
MIX_HALF = D_MODEL // 2
A_HEADS = 8
A_KV_HEADS = 2
A_WINDOW = 128
B_HEADS = 4
B_VDIM = 128
C_WIDTH = MIX_HALF
C_GROUPS = 4
C_CHUNK = 128
D_NOPE = 64
D_ROPE = 64
D_VDIM = 128
D_HEADS = 4
D_Q_RANK = 256
D_KV_RANK = 256
FFN_HIDDEN = 2816
EVEN_IN = 2304
ODD_C = 2 * C_WIDTH

LANES = 128
MXU_DIM = 256
VMEM_LIMIT = 56 * 1024 * 1024
PROJ_SUB_ROWS = 256
ATTN_KEY_CHUNK = 256
LOG2E = math.log2(math.e)

EV_QA, EV_QB, EV_KA, EV_KB, EV_VA, EV_VB = 0, 4, 8, 9, 13, 14
EV_NORM_TILES = 13
OD_C, OD_Q, OD_K, OD_V = 0, 4, 8, 12
OD_OUT = 2048
OD_IN_PAD = 1664


def _dot(a, b):
    return jnp.dot(a, b, preferred_element_type=F32)


def _dot_nt(a, b):
    return lax.dot_general(a, b, (((1,), (1,)), ((), ())), preferred_element_type=F32)


def _lane_is_a(shape):
    lane = lax.broadcasted_iota(jnp.int32, shape, len(shape) - 1)
    return (lane // HALF) % 2 == 0


def _modulated_norm(x, g, shift, scale):
    ms = jnp.mean(x * x, axis=-1, keepdims=True)
    return (x * lax.rsqrt(ms + EPS) * g) * (1.0 + scale) + shift


def _norm_rope_tiles(t, m_ref, gains_ref, n_tiles, cos, sin):
    out = []
    j = 0
    while j < n_tiles:
        width = MXU_DIM if j + 1 < n_tiles else LANES
        tp = t[:, j * LANES:j * LANES + width]
        msq = _dot((tp * tp).astype(BF16), m_ref[:width, :width])
        for q in range(width // LANES):
            tt = tp[:, q * LANES:(q + 1) * LANES]
            r = lax.rsqrt(msq[:, q * LANES:(q + 1) * LANES] + EPS)
            tn = tt * r * gains_ref[j + q:j + q + 1, :]
            if cos is not None:
                tn = tn * cos + pltpu.roll(tn, 2 * HALF, 1) * sin
            out.append(tn.astype(BF16))
        j += width // LANES
    return out


def _adaln_kernel(cv_ref, w_ref, b_ref, o_ref):
    a = cv_ref[...]
    a = a * jax.nn.sigmoid(a)
    w = w_ref[...]
    a_hi = a.astype(BF16)
    a_lo = (a - a_hi.astype(F32)).astype(BF16)
    w_hi = w.astype(BF16)
    w_lo = (w - w_hi.astype(F32)).astype(BF16)
    o_ref[...] = _dot(a_hi, w_hi) + _dot(a_hi, w_lo) + _dot(a_lo, w_hi) + b_ref[...]


def _adaln(cv, ada_w, ada_b):
    tn = 768
    n6 = 6 * D_MODEL
    rows = cv.shape[0]
    return pl.pallas_call(
        _adaln_kernel,
        grid=(DEPTH, n6 // tn),
        in_specs=[
            pl.BlockSpec((rows, D_MODEL), lambda l, j: (0, 0)),
            pl.BlockSpec((None, D_MODEL, tn), lambda l, j: (l, 0, j)),
            pl.BlockSpec((None, 1, tn), lambda l, j: (l, 0, j)),
        ],
        out_specs=pl.BlockSpec((None, rows, tn), lambda l, j: (l, 0, j)),
        out_shape=jax.ShapeDtypeStruct((DEPTH, rows, n6), F32),
        compiler_params=pltpu.CompilerParams(
            dimension_semantics=("arbitrary", "arbitrary"), vmem_limit_bytes=VMEM_LIMIT),
        name="adaln",
    )(cv, ada_w, ada_b.reshape(DEPTH, 1, n6))


def _proj_even_kernel(*refs, rope):
    if rope:
        x_ref, mod_ref, g_ref, w_ref, m_ref, gains_ref, cos_ref, sin_ref, o_ref = refs
    else:
        x_ref, mod_ref, g_ref, w_ref, m_ref, gains_ref, o_ref = refs
    for r0 in range(0, x_ref.shape[0], PROJ_SUB_ROWS):
        rows = slice(r0, r0 + PROJ_SUB_ROWS)
        cos, sin = (cos_ref[rows, :], sin_ref[rows, :]) if rope else (None, None)
        h = _modulated_norm(x_ref[rows, :], g_ref[...], mod_ref[:, 0:D_MODEL], mod_ref[:, D_MODEL:2 * D_MODEL])
        t = _dot(h.astype(BF16), w_ref[...])
        tiles = _norm_rope_tiles(t, m_ref, gains_ref, EV_NORM_TILES, cos, sin)
        for j, tile in enumerate(tiles):
            o_ref[rows, j * LANES:(j + 1) * LANES] = tile
        o_ref[rows, EV_NORM_TILES * LANES:] = t[:, EV_NORM_TILES * LANES:].astype(BF16)


def _const_spec(shape):
    nd = len(shape)
    return pl.BlockSpec(shape, lambda *_: (0,) * nd, pipeline_mode=pl.Buffered(1))


def _proj_even(xs, mods, mod_row, g, w, m256, gains, cos, sin, tm):
    bsz, n, _ = xs.shape
    rope = cos is not None
    in_specs = [
        pl.BlockSpec((None, tm, D_MODEL), lambda b, i: (b, i, 0)),
        pl.BlockSpec((None, 1, 6 * D_MODEL), lambda b, i: (mod_row(b), 0, 0)),
        _const_spec((1, D_MODEL)),
        _const_spec((D_MODEL, EVEN_IN)),
        _const_spec((MXU_DIM, MXU_DIM)),
        _const_spec(gains.shape),
    ]
    args = [xs, mods, g, w, m256, gains]
    if rope:
        in_specs += [pl.BlockSpec((tm, LANES), lambda b, i: (i, 0))] * 2
        args += [cos, sin]
    return pl.pallas_call(
        functools.partial(_proj_even_kernel, rope=rope),
        grid=(bsz, n // tm),
        in_specs=in_specs,
        out_specs=pl.BlockSpec((None, tm, EVEN_IN), lambda b, i: (b, i, 0)),
        out_shape=jax.ShapeDtypeStruct((bsz, n, EVEN_IN), BF16),
        compiler_params=pltpu.CompilerParams(
            dimension_semantics=("parallel", "parallel"), vmem_limit_bytes=VMEM_LIMIT),
        name="proj_even_lat" if rope else "proj_even_ctx",
    )(*args)


def _sub_lane_tiles(s, m):
    return jnp.concatenate([s[:, t * LANES:(t + 1) * LANES] - m for t in range(s.shape[1] // LANES)], axis=1)


def _attn_a_kernel(*refs, local, tq, n_units, nq):
    grp = A_HEADS // A_KV_HEADS
    refs = list(refs)
    m_scr = refs.pop()
    s_scr = refs.pop()
    o_ref = refs.pop()
    sink_ref, q_ref = refs[:2]
    j = pl.program_id(0)
    if local:
        kp_ref, kc_ref, kn_ref, vp_ref, vc_ref, vn_ref, band_ref, kx_ref, vx_ref = refs[2:]
        i = jnp.minimum(j, n_units - 1) % nq
        neg_first = jnp.where(i == 0, NEG, 0.0)
        neg_last = jnp.where(i == nq - 1, NEG, 0.0)
        w = A_WINDOW
        srcs = [(kp_ref, vp_ref, band_ref[:, :w] + neg_first), (kc_ref, vc_ref, band_ref[:, w:w + tq]),
                (kn_ref, vn_ref, band_ref[:, w + tq:] + neg_last), (kx_ref, vx_ref, None)]
    else:
        kx_ref, vx_ref = refs[2:]
        srcs = [(kx_ref, vx_ref, None)]

    @pl.when(j == 0)
    def _():
        s_scr[...] = jnp.zeros(s_scr.shape, F32)
        m_scr[...] = jnp.zeros(m_scr.shape, F32)

    is_a = _lane_is_a((1, LANES))
    q = q_ref[...]
    zero = jnp.zeros((tq, LANES), BF16)
    qs = jnp.concatenate(
        [jnp.where(is_a if kvh == 0 else jnp.logical_not(is_a), q[:, t * LANES:(t + 1) * LANES], zero)
         for kvh in range(A_KV_HEADS) for t in range(grp)], axis=0)
    sink = jnp.concatenate([jnp.full((tq, LANES), sink_ref[h] * LOG2E, F32) for h in range(A_HEADS)], axis=0)
    m_prev = m_scr[...]
    acc = l_part = m_part = None
    col = 0
    for k_ref, v_ref, bias in srcs:
        sz = k_ref.shape[0]
        e = jnp.exp2(_sub_lane_tiles(s_scr[:, col:col + sz], m_prev))
        lc = _lane_fold(e, jnp.add)
        l_part = lc if l_part is None else l_part + lc
        pv = _dot(e.astype(BF16), v_ref[...])
        acc = pv if acc is None else acc + pv
        s_new = _dot_nt(qs, k_ref[...])
        if bias is not None:
            s_new = (s_new.reshape(A_HEADS, tq, sz) + bias[None]).reshape(A_HEADS * tq, sz)
        s_scr[:, col:col + sz] = s_new
        mc = _lane_fold(s_new, jnp.maximum)
        m_part = mc if m_part is None else jnp.maximum(m_part, mc)
        col += sz
    m_new = jnp.broadcast_to(jnp.max(m_part, axis=-1, keepdims=True), m_scr.shape)
    m_scr[...] = jnp.maximum(m_new, sink)
    l = jnp.broadcast_to(jnp.sum(l_part, axis=-1, keepdims=True), m_scr.shape) + jnp.exp2(sink - m_prev)
    o = acc / l
    low = lax.broadcasted_iota(jnp.int32, (1, LANES), 1) < HEAD_DIM
    for t in range(grp):
        pair = jnp.where(low, o[t * tq:(t + 1) * tq], o[(grp + t) * tq:(grp + t + 1) * tq])
        o_ref[:, t * LANES:(t + 1) * LANES] = pair.astype(BF16)


def _attn_a(qsrc, pc, sink, local, tq):
    bsz, n, _ = qsrc.shape
    lctx = pc.shape[1]
    nq = n // tq
    r = tq // A_WINDOW
    nblk = n // A_WINDOW
    qw = MIX_HALF
    n_units = bsz * nq

    def unit(u):
        return u // nq, u % nq

    def cur(j):
        return unit(jnp.minimum(j, n_units - 1))

    def prev(j):
        return unit(jnp.maximum(j - 1, 0))

    def before(which, col):
        return lambda j: (which(j)[0], jnp.maximum(which(j)[1] * r - 1, 0), col)

    def at(which, col):
        return lambda j: (which(j)[0], which(j)[1], col)

    def after(which, col):
        return lambda j: (which(j)[0], jnp.minimum((which(j)[1] + 1) * r, nblk - 1), col)

    in_specs = [pl.BlockSpec(memory_space=pltpu.SMEM), pl.BlockSpec((None, tq, qw), at(cur, 0))]
    args = [sink, qsrc]
    n_keys = lctx
    if local:
        for which, col in ((cur, EV_KA), (prev, EV_VA)):
            in_specs += [pl.BlockSpec((None, A_WINDOW, LANES), before(which, col)),
                         pl.BlockSpec((None, tq, LANES), at(which, col)),
                         pl.BlockSpec((None, A_WINDOW, LANES), after(which, col))]
            args += [qsrc, qsrc, qsrc]
        off = np.arange(tq + 2 * A_WINDOW)[None, :] - A_WINDOW
        band = np.where(np.abs(np.arange(tq)[:, None] - off) <= A_WINDOW, 0.0, NEG).astype(np.float32)
        in_specs.append(_const_spec(band.shape))
        args.append(jnp.asarray(band))
        n_keys += tq + 2 * A_WINDOW
    in_specs += [pl.BlockSpec((None, lctx, LANES), lambda j: (cur(j)[0], 0, EV_KA)),
                 pl.BlockSpec((None, lctx, LANES), lambda j: (prev(j)[0], 0, EV_VA))]
    args += [pc, pc]
    rows = A_HEADS * tq
    return pl.pallas_call(
        functools.partial(_attn_a_kernel, local=local, tq=tq, n_units=n_units, nq=nq),
        grid=(n_units + 1,),
        in_specs=in_specs,
        out_specs=pl.BlockSpec((None, tq, qw), at(prev, 0)),
        out_shape=jax.ShapeDtypeStruct((bsz, n, qw), BF16),
        scratch_shapes=[pltpu.VMEM((rows, n_keys), F32), pltpu.VMEM((rows, LANES), F32)],
        compiler_params=pltpu.CompilerParams(
            dimension_semantics=("arbitrary",), vmem_limit_bytes=VMEM_LIMIT),
        name="attn_a_lat" if local else "attn_a_ctx",
    )(*args)


def _lane_fold(x, op):
    out = x[:, :LANES]
    for t in range(1, x.shape[1] // LANES):
        out = op(out, x[:, t * LANES:(t + 1) * LANES])
    return out


def _attn_full_kernel(*refs, diff, has_lat, lam_init, tq, ck):
    refs = list(refs)
    m_scr = refs.pop()
    s_scr = refs.pop()
    o_ref = refs.pop()
    if diff:
        lamv_ref = refs.pop(0)
        subln_ref = refs.pop()
    q_ref = refs.pop(0)
    srcs = [(refs[0], refs[1]), (refs[2], refs[3])] if has_lat else [(refs[0], refs[1])]

    @pl.when(pl.program_id(0) == 0)
    def _():
        s_scr[...] = jnp.zeros(s_scr.shape, F32)
        m_scr[...] = jnp.zeros(m_scr.shape, F32)

    q = q_ref[...]
    if diff:
        is_a = _lane_is_a((1, LANES))
        zero = jnp.zeros_like(q)
        qs = jnp.concatenate([jnp.where(is_a, q, zero), jnp.where(is_a, zero, q)], axis=0)
    else:
        qs = q
    m_prev = m_scr[...]
    acc = l_part = m_part = None
    col = 0
    for k_ref, v_ref in srcs:
        nk = k_ref.shape[0]
        for st in range(0, nk, ck):
            sz = min(ck, nk - st)
            s_old = s_scr[:, col:col + sz]
            e = jnp.exp2(jnp.concatenate(
                [s_old[:, t * LANES:(t + 1) * LANES] - m_prev for t in range(sz // LANES)], axis=1))
            lc = _lane_fold(e, jnp.add)
            l_part = lc if l_part is None else l_part + lc
            pv = _dot(e.astype(BF16), v_ref[st:st + sz, :])
            acc = pv if acc is None else acc + pv
            s_new = _dot_nt(qs, k_ref[st:st + sz, :])
            s_scr[:, col:col + sz] = s_new
            mc = _lane_fold(s_new, jnp.maximum)
            m_part = mc if m_part is None else jnp.maximum(m_part, mc)
            col += sz
    m_scr[...] = jnp.broadcast_to(jnp.max(m_part, axis=-1, keepdims=True), m_scr.shape)
    o = acc / jnp.sum(l_part, axis=-1, keepdims=True)
    if diff:
        lv = lamv_ref[...]
        lam = (jnp.exp(jnp.sum(lv[0:1] * lv[1:2], axis=-1, keepdims=True))
               - jnp.exp(jnp.sum(lv[2:3] * lv[3:4], axis=-1, keepdims=True)) + lam_init)
        o = o[:tq] - lam * o[tq:]
        ms = jnp.mean(o * o, axis=-1, keepdims=True)
        o = (o * lax.rsqrt(ms + EPS) * subln_ref[...]) * (1.0 - lam_init)
    o_ref[...] = o.astype(BF16)


def _attn_full(qsrc, q_col, lat, ctx, k_col, v_col, nheads, tq, diff, lamv=None, subln=None, lam_init=0.0,
               ctx_cols=None, name="attn_full"):
    bsz, n, _ = qsrc.shape
    nq = n // tq
    n_tiles = bsz * nheads * nq

    def tile_of(j):
        return j // (nheads * nq), (j // nq) % nheads, j % nq

    def cur(j):
        return tile_of(jnp.minimum(j, n_tiles - 1))

    def prev(j):
        return tile_of(jnp.maximum(j - 1, 0))

    def q_map(j):
        b, h, i = cur(j)
        return b, i, q_col + h

    def k_map(j, col):
        b, h, _ = cur(j)
        return b, 0, col + h

    def v_map(j, col):
        b, h, _ = prev(j)
        return b, 0, col + h

    def o_map(j):
        b, h, i = prev(j)
        return b, i, h

    in_specs, args = [], []
    if diff:
        in_specs.append(_const_spec(lamv.shape))
        args.append(lamv)
    in_specs.append(pl.BlockSpec((None, tq, LANES), q_map))
    args.append(qsrc)
    ckc, cvc = (k_col, v_col) if ctx_cols is None else ctx_cols
    for src, kc, vc in ([(lat, k_col, v_col)] if lat is not None else []) + [(ctx, ckc, cvc)]:
        nk = src.shape[1]
        in_specs += [pl.BlockSpec((None, nk, LANES), functools.partial(k_map, col=kc)),
                     pl.BlockSpec((None, nk, LANES), functools.partial(v_map, col=vc))]
        args += [src, src]
    if diff:
        in_specs.append(_const_spec(subln.shape))
        args.append(subln)
    rows = 2 * tq if diff else tq
    n_keys = ctx.shape[1] + (lat.shape[1] if lat is not None else 0)
    return pl.pallas_call(
        functools.partial(_attn_full_kernel, diff=diff, has_lat=lat is not None, lam_init=lam_init, tq=tq,
                          ck=ATTN_KEY_CHUNK),
        grid=(n_tiles + 1,),
        in_specs=in_specs,
        out_specs=pl.BlockSpec((None, tq, LANES), o_map),
        out_shape=jax.ShapeDtypeStruct((bsz, n, nheads * LANES), BF16),
        scratch_shapes=[pltpu.VMEM((rows, n_keys), F32), pltpu.VMEM((rows, LANES), F32)],
        compiler_params=pltpu.CompilerParams(
            dimension_semantics=("arbitrary",), vmem_limit_bytes=VMEM_LIMIT),
        name=name,
    )(*args)


FFN_CHUNK = MXU_DIM


def _out_ffn_kernel(x_ref, ma_ref, mb_ref, mod_ref, g_ref, woa_ref, wob_ref, wi_ref, wo2_ref, o_ref):
    dm = D_MODEL
    mix = _dot(ma_ref[...], woa_ref[...]) + _dot(mb_ref[...], wob_ref[...])
    x1 = x_ref[...] + mod_ref[:, 2 * dm:3 * dm] * mix
    h2 = _modulated_norm(x1, g_ref[...], mod_ref[:, 3 * dm:4 * dm], mod_ref[:, 4 * dm:5 * dm]).astype(BF16)
    acc = jnp.zeros(x1.shape, F32)
    for c in range(FFN_HIDDEN // FFN_CHUNK):
        lo = c * FFN_CHUNK
        gate = _dot(h2, wi_ref[:, lo:lo + FFN_CHUNK])
        up = _dot(h2, wi_ref[:, FFN_HIDDEN + lo:FFN_HIDDEN + lo + FFN_CHUNK])
        act = (gate * jax.nn.sigmoid(gate) * up).astype(BF16)
        acc = acc + _dot(act, wo2_ref[lo:lo + FFN_CHUNK, :])
    o_ref[...] = x1 + mod_ref[:, 5 * dm:6 * dm] * acc


def _out_ffn(xs, ma, mb, mods, mod_row, g, woa, wob, wi, wo2, layer, tm, name):
    bsz, n, _ = xs.shape

    def layer_spec(shape):
        return pl.BlockSpec((None,) + shape, lambda *_: (layer, 0, 0), pipeline_mode=pl.Buffered(1))

    return pl.pallas_call(
        _out_ffn_kernel,
        grid=(bsz, n // tm),
        in_specs=[
            pl.BlockSpec((None, tm, D_MODEL), lambda b, i: (b, i, 0)),
            pl.BlockSpec((None, tm, woa.shape[0]), lambda b, i: (b, i, 0)),
            pl.BlockSpec((None, tm, wob.shape[0]), lambda b, i: (b, i, 0)),
            pl.BlockSpec((None, 1, 6 * D_MODEL), lambda b, i: (mod_row(b), 0, 0)),
            _const_spec((1, D_MODEL)),
            _const_spec(woa.shape),
            _const_spec(wob.shape),
            layer_spec((D_MODEL, 2 * FFN_HIDDEN)),
            layer_spec((FFN_HIDDEN, D_MODEL)),
        ],
        out_specs=pl.BlockSpec((None, tm, D_MODEL), lambda b, i: (b, i, 0)),
        out_shape=jax.ShapeDtypeStruct(xs.shape, F32),
        compiler_params=pltpu.CompilerParams(
            dimension_semantics=("parallel", "parallel"), vmem_limit_bytes=VMEM_LIMIT),
        name=name,
    )(xs, ma, mb, mods, g, woa, wob, wi, wo2)


def _mla_keys_values(ckv, kpe_tile, kvan_ref, wukv_ref, m_ref, kgain_ref, cos, sin):
    ms = jnp.mean(ckv * ckv, axis=-1, keepdims=True)
    ckv_n = (ckv * lax.rsqrt(ms + EPS) * kvan_ref[...]).astype(BF16)
    kv = _dot(ckv_n, wukv_ref[...])
    nk = D_HEADS * LANES
    kraw = jnp.concatenate(
        [kv[:, h * LANES:(h + 1) * LANES] + kpe_tile for h in range(D_HEADS)], axis=1)
    ktiles = _norm_rope_tiles(kraw, m_ref, kgain_ref, D_HEADS, cos, sin)
    return ktiles, kv[:, nk:].astype(BF16)


def _proj_odd_kernel(x_ref, mod_ref, g_ref, w_ref, vn_ref, ws_ref, bs_ref, qan_ref, kvan_ref, wuq_ref,
                     wukv_ref, m_ref, qgain_ref, kgain_ref, cos_ref, sin_ref, o_ref, *, tm):
    for r0 in range(0, tm, PROJ_SUB_ROWS):
        rows = slice(r0, r0 + PROJ_SUB_ROWS)
        h = _modulated_norm(x_ref[rows, :], g_ref[...], mod_ref[:, 0:D_MODEL], mod_ref[:, D_MODEL:2 * D_MODEL])
        t = _dot(h.astype(BF16), w_ref[...])
        cos, sin = cos_ref[rows, :], sin_ref[rows, :]
        uv = t[:, :ODD_C]
        uv = 0.5 * uv * (1.0 + lax.erf(uv * math.sqrt(0.5)))
        u = uv[:, :C_WIDTH]
        v = uv[:, C_WIDTH:]
        ms = jnp.mean(v * v, axis=-1, keepdims=True)
        v = (v * lax.rsqrt(ms + EPS) * vn_ref[...]).astype(BF16)
        for c0 in range(0, PROJ_SUB_ROWS, C_CHUNK):
            for gidx in range(C_GROUPS):
                l0 = gidx * LANES
                vs = _dot(ws_ref[gidx], v[c0:c0 + C_CHUNK, l0:l0 + LANES]) + bs_ref[gidx]
                o_ref[r0 + c0:r0 + c0 + C_CHUNK, l0:l0 + LANES] = (
                    u[c0:c0 + C_CHUNK, l0:l0 + LANES] * vs).astype(BF16)
        cq = t[:, ODD_C:ODD_C + D_Q_RANK]
        ms = jnp.mean(cq * cq, axis=-1, keepdims=True)
        cq_n = (cq * lax.rsqrt(ms + EPS) * qan_ref[...]).astype(BF16)
        qtiles = _norm_rope_tiles(_dot(cq_n, wuq_ref[...]), m_ref, qgain_ref, D_HEADS, cos, sin)
        for hh, tile in enumerate(qtiles):
            o_ref[rows, (OD_Q + hh) * LANES:(OD_Q + hh + 1) * LANES] = tile
        k0 = ODD_C + D_Q_RANK
        ktiles, vals = _mla_keys_values(t[:, k0:k0 + D_KV_RANK], t[:, k0 + D_KV_RANK:], kvan_ref, wukv_ref,
                                        m_ref, kgain_ref, cos, sin)
        for hh, tile in enumerate(ktiles):
            o_ref[rows, (OD_K + hh) * LANES:(OD_K + hh + 1) * LANES] = tile
        o_ref[rows, OD_V * LANES:] = vals


def _proj_odd_ctx_kernel(x_ref, mod_ref, g_ref, w_ref, kvan_ref, wukv_ref, m_ref, kgain_ref, o_ref):
    h = _modulated_norm(x_ref[...], g_ref[...], mod_ref[:, 0:D_MODEL], mod_ref[:, D_MODEL:2 * D_MODEL])
    t = _dot(h.astype(BF16), w_ref[...])
    ktiles, vals = _mla_keys_values(t[:, :D_KV_RANK], t[:, D_KV_RANK:], kvan_ref, wukv_ref, m_ref, kgain_ref,
                                    None, None)
    for hh, tile in enumerate(ktiles):
        o_ref[:, hh * LANES:(hh + 1) * LANES] = tile
    o_ref[:, D_HEADS * LANES:] = vals


def _proj_odd(xs, mods, g, w, vn, ws, bs, qan, kvan, wuq, wukv, m256, qgain, kgain, cos, sin, tm):
    bsz, n, _ = xs.shape
    consts = [g, w, vn, ws, bs, qan, kvan, wuq, wukv, m256, qgain, kgain]
    return pl.pallas_call(
        functools.partial(_proj_odd_kernel, tm=tm),
        grid=(bsz, n // tm),
        in_specs=[pl.BlockSpec((None, tm, D_MODEL), lambda b, i: (b, i, 0)),
                  pl.BlockSpec((None, 1, 6 * D_MODEL), lambda b, i: (b, 0, 0))]
        + [_const_spec(a.shape) for a in consts]
        + [pl.BlockSpec((tm, LANES), lambda b, i: (i, 0))] * 2,
        out_specs=pl.BlockSpec((None, tm, OD_OUT), lambda b, i: (b, i, 0)),
        out_shape=jax.ShapeDtypeStruct((bsz, n, OD_OUT), BF16),
        compiler_params=pltpu.CompilerParams(
            dimension_semantics=("parallel", "parallel"), vmem_limit_bytes=VMEM_LIMIT),
        name="proj_odd_lat",
    )(xs, mods, *consts, cos, sin)


def _proj_odd_ctx(xs, mods, mod_row, g, w, kvan, wukv, m256, kgain, tm):
    bsz, n, _ = xs.shape
    consts = [g, w, kvan, wukv, m256, kgain]
    width = 2 * D_HEADS * LANES
    return pl.pallas_call(
        _proj_odd_ctx_kernel,
        grid=(bsz, n // tm),
        in_specs=[pl.BlockSpec((None, tm, D_MODEL), lambda b, i: (b, i, 0)),
                  pl.BlockSpec((None, 1, 6 * D_MODEL), lambda b, i: (mod_row(b), 0, 0))]
        + [_const_spec(a.shape) for a in consts],
        out_specs=pl.BlockSpec((None, tm, width), lambda b, i: (b, i, 0)),
        out_shape=jax.ShapeDtypeStruct((bsz, n, width), BF16),
        compiler_params=pltpu.CompilerParams(
            dimension_semantics=("parallel", "parallel"), vmem_limit_bytes=VMEM_LIMIT),
        name="proj_odd_ctx",
    )(xs, mods, *consts)


def _pair_cols(ca, cb):
    lo = np.arange(HALF)
    return np.concatenate([ca + lo, cb + lo, ca + HALF + lo, cb + HALF + lo])


def _pair_gain(ga, gb, scale=1.0):
    return jnp.concatenate([ga[:HALF], gb[:HALF], ga[HALF:], gb[HALF:]]).astype(F32) * scale


def _even_col_perm():
    a_q, b_q = 0, A_HEADS * HEAD_DIM
    a_k = 2 * MIX_HALF
    a_v = a_k + A_KV_HEADS * HEAD_DIM
    b_k = a_v + A_KV_HEADS * HEAD_DIM
    b_v = b_k + B_HEADS * 2 * HEAD_DIM
    grp = A_HEADS // A_KV_HEADS
    cols = [_pair_cols(a_q + j * HEAD_DIM, a_q + (grp + j) * HEAD_DIM) for j in range(grp)]
    cols += [_pair_cols(b_q + h * LANES, b_q + h * LANES + HEAD_DIM) for h in range(B_HEADS)]
    cols += [_pair_cols(a_k, a_k + HEAD_DIM)]
    cols += [_pair_cols(b_k + h * LANES, b_k + h * LANES + HEAD_DIM) for h in range(B_HEADS)]
    cols += [np.arange(a_v, b_k), np.arange(b_v, EVEN_IN)]
    return np.concatenate(cols)


def _mixer_a_out_rows():
    grp = A_HEADS // A_KV_HEADS
    return np.concatenate([np.arange(HEAD_DIM) + h * HEAD_DIM for t in range(grp) for h in (t, grp + t)])


def _block_mean_matrix():
    seg = (np.arange(MXU_DIM) // LANES) * 2 + (np.arange(MXU_DIM) // HALF) % 2
    return jnp.asarray((seg[:, None] == seg[None, :]).astype(np.float32) / HEAD_DIM, dtype=BF16)


def _rope_tables(n_tokens):
    rows = n_tokens // GRID_W
    inv = ROPE_THETA ** (-jnp.arange(ROPE_AXIS_PAIRS, dtype=F32) / ROPE_AXIS_PAIRS)
    ang_r = jnp.arange(rows, dtype=jnp.int32).astype(F32)[:, None] * inv
    ang_c = jnp.arange(GRID_W, dtype=jnp.int32).astype(F32)[:, None] * inv

    def per_token(fn):
        by_row = jnp.broadcast_to(fn(ang_r)[:, None, :], (rows, GRID_W, ROPE_AXIS_PAIRS))
        by_col = jnp.broadcast_to(fn(ang_c)[None, :, :], (rows, GRID_W, ROPE_AXIS_PAIRS))
        return jnp.concatenate([by_row, by_col], axis=-1).reshape(n_tokens, 2 * ROPE_AXIS_PAIRS)

    c, s = per_token(jnp.cos), per_token(jnp.sin)
    one, zero = jnp.ones_like(c), jnp.zeros_like(c)
    even = (jnp.concatenate([c, c, c, c], axis=1), jnp.concatenate([-s, -s, s, s], axis=1))
    odd = (jnp.concatenate([one, c, one, c], axis=1), jnp.concatenate([zero, -s, zero, s], axis=1))
    return even, odd


def _pad_rows(a, rows):
    return jnp.concatenate([a, jnp.zeros((rows - a.shape[0],) + a.shape[1:], a.dtype)], axis=0)


def kernel(x, c, ctx, c_ctx, norm1_g, norm2_g, ada_w, ada_b, mix_w_out, ffn_w_in, ffn_w_out, ev_w_in, ev_qnorm_a, ev_knorm_a, ev_sink, ev_qnorm_b, ev_knorm_b, ev_lam_q1, ev_lam_k1, ev_lam_q2, ev_lam_k2, ev_subln, od_w_in, od_c_vnorm, od_c_ws, od_c_bs, od_qa_norm, od_kva_norm, od_w_uq, od_w_ukv, od_qnorm_nope, od_knorm_nope, od_qnorm_rope, od_knorm_rope):
    bsz, seq, dm = x.shape
    lctx = ctx.shape[1]
    assert (dm, DEPTH) == (D_MODEL, 2) and seq % 512 == 0 and lctx % 256 == 0
    ctx_row = bsz

    cv = _pad_rows(jnp.concatenate([c, c_ctx[None, :]], axis=0), 16)
    mods = _adaln(cv, ada_w, ada_b).reshape(DEPTH, 16, 1, 6 * dm)
    (cos_e, sin_e), (cos_o, sin_o) = _rope_tables(seq)
    m256 = _block_mean_matrix()
    lat_row = lambda b: b
    ctx_mod = lambda b: ctx_row

    w_in0 = ev_w_in[0][:, _even_col_perm()].astype(BF16)
    sa = HEAD_DIM ** -0.5 * LOG2E
    gains0 = _pad_rows(jnp.stack(
        [_pair_gain(ev_qnorm_a[0], ev_qnorm_a[0], sa)] * 4 + [_pair_gain(ev_qnorm_b[0], ev_qnorm_b[0], sa)] * 4
        + [_pair_gain(ev_knorm_a[0], ev_knorm_a[0])] + [_pair_gain(ev_knorm_b[0], ev_knorm_b[0])] * 4), 16)
    g1 = norm1_g[0][None, :]
    pl0 = _proj_even(x, mods[0], lat_row, g1, w_in0, m256, gains0, cos_e, sin_e, tm=1024)
    ctx_flat = ctx.reshape(1, bsz * lctx, dm)
    pc0 = _proj_even(ctx_flat, mods[0], ctx_mod, g1, w_in0, m256, gains0, None, None, tm=512)
    pc0 = pc0.reshape(bsz, lctx, EVEN_IN)

    lamv = _pad_rows(jnp.stack([ev_lam_q1[0], ev_lam_k1[0], ev_lam_q2[0], ev_lam_k2[0]]), 8)
    subln = ev_subln[0][None, :]
    lam_init = 0.8 - 0.6 * math.exp(-0.3 * 0)
    sink = ev_sink[0].astype(F32)
    out_a = _attn_a(pl0, pc0, sink, local=True, tq=2 * A_WINDOW)
    out_ac = _attn_a(pc0, pc0, sink, local=False, tq=lctx)
    out_b = _attn_full(pl0, EV_QB, pl0, pc0, EV_KB, EV_VB, B_HEADS, 512, True, lamv, subln, lam_init,
                       name="attn_b_lat")
    out_bc = _attn_full(pc0, EV_QB, None, pc0, EV_KB, EV_VB, B_HEADS, lctx, True, lamv, subln, lam_init,
                        name="attn_b_ctx")

    woa0 = mix_w_out[0][_mixer_a_out_rows(), :].astype(BF16)
    wob0 = mix_w_out[0][MIX_HALF:, :].astype(BF16)
    wi = ffn_w_in.astype(BF16)
    wo2 = ffn_w_out.astype(BF16)
    g2 = norm2_g[0][None, :]
    x1 = _out_ffn(x, out_a, out_b, mods[0], lat_row, g2, woa0, wob0, wi, wo2, 0, 512, "out_ffn0_lat")
    xc1 = _out_ffn(ctx_flat, out_ac.reshape(1, bsz * lctx, -1), out_bc.reshape(1, bsz * lctx, -1), mods[0],
                   ctx_mod, g2, woa0, wob0, wi, wo2, 0, 512, "out_ffn0_ctx")

    w1 = od_w_in[0]
    z32 = jnp.zeros((dm, HALF), F32)
    kv0 = ODD_C + D_Q_RANK
    pe0 = kv0 + D_KV_RANK
    kv_cols = jnp.concatenate([w1[:, kv0:pe0], z32, w1[:, pe0:pe0 + HALF], z32, w1[:, pe0 + HALF:]], axis=1)
    w_in1 = jnp.concatenate([w1[:, :kv0], kv_cols], axis=1).astype(BF16)
    w_in1_ctx = kv_cols.astype(BF16)
    wuq = od_w_uq[0][:, np.concatenate(
        [_pair_cols(h * LANES, h * LANES + D_NOPE) for h in range(D_HEADS)])].astype(BF16)
    per_head = D_NOPE + D_VDIM
    zcol = D_HEADS * per_head
    lo = np.arange(HALF)
    kcols = np.concatenate([np.concatenate([h * per_head + lo, np.full(HALF, zcol), h * per_head + HALF + lo,
                                            np.full(HALF, zcol)]) for h in range(D_HEADS)])
    vcols = np.concatenate([h * per_head + D_NOPE + np.arange(D_VDIM) for h in range(D_HEADS)])
    wukv = jnp.concatenate([od_w_ukv[0], jnp.zeros((D_KV_RANK, 1), F32)], axis=1)[
        :, np.concatenate([kcols, vcols])].astype(BF16)
    sd = (D_NOPE + D_ROPE) ** -0.5 * LOG2E
    qgain = _pad_rows(jnp.stack([_pair_gain(od_qnorm_nope[0], od_qnorm_rope[0], sd)] * D_HEADS), 8)
    kgain = _pad_rows(jnp.stack([_pair_gain(od_knorm_nope[0], od_knorm_rope[0])] * D_HEADS), 8)
    ws = od_c_ws[0].astype(BF16)
    bs = jnp.broadcast_to(od_c_bs[0][:, :, None], (C_GROUPS, C_CHUNK, LANES)).astype(F32)
    g1 = norm1_g[1][None, :]
    kvan = od_kva_norm[0][None, :]
    pl1 = _proj_odd(x1, mods[1], g1, w_in1, od_c_vnorm[0][None, :], ws, bs, od_qa_norm[0][None, :], kvan,
                    wuq, wukv, m256, qgain, kgain, cos_o, sin_o, tm=1024)
    pc1 = _proj_odd_ctx(xc1, mods[1], ctx_mod, g1, w_in1_ctx, kvan, wukv, m256, kgain, tm=512)
    pc1 = pc1.reshape(bsz, lctx, -1)
    out_d = _attn_full(pl1, OD_Q, pl1, pc1, OD_K, OD_V, D_HEADS, 1024, False, ctx_cols=(0, D_HEADS),
                       name="attn_d_lat")

    wo1 = mix_w_out[1].astype(BF16)
    return _out_ffn(x1, pl1, out_d, mods[1], lat_row, norm2_g[1][None, :], wo1[:MIX_HALF], wo1[MIX_HALF:],
                    wi, wo2, 1, 512, "out_ffn1_lat")
```

```python
import functools
import math

import numpy as np
import jax
import jax.numpy as jnp
from jax import lax
from jax.experimental import pallas as pl
from jax.experimental.pallas import tpu as pltpu

F32 = jnp.float32
BF16 = jnp.bfloat16

D_MODEL = 1024
DEPTH = 2
GRID_W = 64
HEAD_DIM = 64
HALF = HEAD_DIM // 2
ROPE_THETA = 10000.0
ROPE_AXIS_PAIRS = HEAD_DIM // 4
EPS = 1e-6
NEG = -1e30
MIX_HALF = D_MODEL // 2
A_HEADS = 8
A_KV_HEADS = 2
A_WINDOW = 128
B_HEADS = 4
B_VDIM = 128
C_WIDTH = MIX_HALF
C_GROUPS = 4
C_CHUNK = 128
D_NOPE = 64
D_ROPE = 64
D_VDIM = 128
D_HEADS = 4
D_Q_RANK = 256
D_KV_RANK = 256
FFN_HIDDEN = 2816
EVEN_IN = 2304
ODD_C = 2 * C_WIDTH

LANES = 128
MXU_DIM = 256
VMEM_LIMIT = 56 * 1024 * 1024
PROJ_SUB_ROWS = 256
ATTN_KEY_CHUNK = 256
LOG2E = math.log2(math.e)

EV_QA, EV_QB, EV_KA, EV_KB, EV_VA, EV_VB = 0, 4, 8, 9, 13, 14
EV_NORM_TILES = 13
OD_C, OD_Q, OD_K, OD_V = 0, 4, 8, 12
OD_OUT = 2048
OD_IN_PAD = 1664


def _dot(a, b):
    return jnp.dot(a, b, preferred_element_type=F32)


def _dot_nt(a, b):
    return lax.dot_general(a, b, (((1,), (1,)), ((), ())), preferred_element_type=F32)


def _lane_is_a(shape):
    lane = lax.broadcasted_iota(jnp.int32, shape, len(shape) - 1)
    return (lane // HALF) % 2 == 0


def _modulated_norm(x, g, shift, scale):
    ms = jnp.mean(x * x, axis=-1, keepdims=True)
    return (x * lax.rsqrt(ms + EPS) * g) * (1.0 + scale) + shift


def _norm_rope_tiles(t, m_ref, gains_ref, n_tiles, cos, sin):
    out = []
    j = 0
    while j < n_tiles:
        width = MXU_DIM if j + 1 < n_tiles else LANES
        tp = t[:, j * LANES:j * LANES + width]
        msq = _dot((tp * tp).astype(BF16), m_ref[:width, :width])
        for q in range(width // LANES):
            tt = tp[:, q * LANES:(q + 1) * LANES]
            r = lax.rsqrt(msq[:, q * LANES:(q + 1) * LANES] + EPS)
            tn = tt * r * gains_ref[j + q:j + q + 1, :]
            if cos is not None:
                tn = tn * cos + pltpu.roll(tn, 2 * HALF, 1) * sin
            out.append(tn.astype(BF16))
        j += width // LANES
    return out


def _adaln_kernel(cv_ref, w_ref, b_ref, o_ref):
    a = cv_ref[...]
    a = a * jax.nn.sigmoid(a)
    w = w_ref[...]
    a_hi = a.astype(BF16)
    a_lo = (a - a_hi.astype(F32)).astype(BF16)
    w_hi = w.astype(BF16)
    w_lo = (w - w_hi.astype(F32)).astype(BF16)
    o_ref[...] = _dot(a_hi, w_hi) + _dot(a_hi, w_lo) + _dot(a_lo, w_hi) + b_ref[...]


def _adaln(cv, ada_w, ada_b):
    tn = 768
    n6 = 6 * D_MODEL
    rows = cv.shape[0]
    return pl.pallas_call(
        _adaln_kernel,
        grid=(DEPTH, n6 // tn),
        in_specs=[
            pl.BlockSpec((rows, D_MODEL), lambda l, j: (0, 0)),
            pl.BlockSpec((None, D_MODEL, tn), lambda l, j: (l, 0, j)),
            pl.BlockSpec((None, 1, tn), lambda l, j: (l, 0, j)),
        ],
        out_specs=pl.BlockSpec((None, rows, tn), lambda l, j: (l, 0, j)),
        out_shape=jax.ShapeDtypeStruct((DEPTH, rows, n6), F32),
        compiler_params=pltpu.CompilerParams(
            dimension_semantics=("arbitrary", "arbitrary"), vmem_limit_bytes=VMEM_LIMIT),
        name="adaln",
    )(cv, ada_w, ada_b.reshape(DEPTH, 1, n6))


def _proj_even_kernel(*refs, rope):
    if rope:
        x_ref, mod_ref, g_ref, w_ref, m_ref, gains_ref, cos_ref, sin_ref, o_ref = refs
    else:
        x_ref, mod_ref, g_ref, w_ref, m_ref, gains_ref, o_ref = refs
    for r0 in range(0, x_ref.shape[0], PROJ_SUB_ROWS):
        rows = slice(r0, r0 + PROJ_SUB_ROWS)
        cos, sin = (cos_ref[rows, :], sin_ref[rows, :]) if rope else (None, None)
        h = _modulated_norm(x_ref[rows, :], g_ref[...], mod_ref[:, 0:D_MODEL], mod_ref[:, D_MODEL:2 * D_MODEL])
        t = _dot(h.astype(BF16), w_ref[...])
        tiles = _norm_rope_tiles(t, m_ref, gains_ref, EV_NORM_TILES, cos, sin)
        for j, tile in enumerate(tiles):
            o_ref[rows, j * LANES:(j + 1) * LANES] = tile
        o_ref[rows, EV_NORM_TILES * LANES:] = t[:, EV_NORM_TILES * LANES:].astype(BF16)


def _const_spec(shape):
    nd = len(shape)
    return pl.BlockSpec(shape, lambda *_: (0,) * nd, pipeline_mode=pl.Buffered(1))


def _proj_even(xs, mods, mod_row, g, w, m256, gains, cos, sin, tm):
    bsz, n, _ = xs.shape
    rope = cos is not None
    in_specs = [
        pl.BlockSpec((None, tm, D_MODEL), lambda b, i: (b, i, 0)),
        pl.BlockSpec((None, 1, 6 * D_MODEL), lambda b, i: (mod_row(b), 0, 0)),
        _const_spec((1, D_MODEL)),
        _const_spec((D_MODEL, EVEN_IN)),
        _const_spec((MXU_DIM, MXU_DIM)),
        _const_spec(gains.shape),
    ]
    args = [xs, mods, g, w, m256, gains]
    if rope:
        in_specs += [pl.BlockSpec((tm, LANES), lambda b, i: (i, 0))] * 2
        args += [cos, sin]
    return pl.pallas_call(
        functools.partial(_proj_even_kernel, rope=rope),
        grid=(bsz, n // tm),
        in_specs=in_specs,
        out_specs=pl.BlockSpec((None, tm, EVEN_IN), lambda b, i: (b, i, 0)),
        out_shape=jax.ShapeDtypeStruct((bsz, n, EVEN_IN), BF16),
        compiler_params=pltpu.CompilerParams(
            dimension_semantics=("parallel", "parallel"), vmem_limit_bytes=VMEM_LIMIT),
        name="proj_even_lat" if rope else "proj_even_ctx",
    )(*args)


def _sub_lane_tiles(s, m):
    return jnp.concatenate([s[:, t * LANES:(t + 1) * LANES] - m for t in range(s.shape[1] // LANES)], axis=1)


def _attn_a_kernel(*refs, local, tq, n_units, nq):
    grp = A_HEADS // A_KV_HEADS
    refs = list(refs)
    m_scr = refs.pop()
    s_scr = refs.pop()
    o_ref = refs.pop()
    sink_ref, q_ref = refs[:2]
    j = pl.program_id(0)
    if local:
        kp_ref, kc_ref, kn_ref, vp_ref, vc_ref, vn_ref, band_ref, kx_ref, vx_ref = refs[2:]
        i = jnp.minimum(j, n_units - 1) % nq
        neg_first = jnp.where(i == 0, NEG, 0.0)
        neg_last = jnp.where(i == nq - 1, NEG, 0.0)
        w = A_WINDOW
        srcs = [(kp_ref, vp_ref, band_ref[:, :w] + neg_first), (kc_ref, vc_ref, band_ref[:, w:w + tq]),
                (kn_ref, vn_ref, band_ref[:, w + tq:] + neg_last), (kx_ref, vx_ref, None)]
    else:
        kx_ref, vx_ref = refs[2:]
        srcs = [(kx_ref, vx_ref, None)]

    @pl.when(j == 0)
    def _():
        s_scr[...] = jnp.zeros(s_scr.shape, F32)
        m_scr[...] = jnp.zeros(m_scr.shape, F32)

    is_a = _lane_is_a((1, LANES))
    q = q_ref[...]
    zero = jnp.zeros((tq, LANES), BF16)
    qs = jnp.concatenate(
        [jnp.where(is_a if kvh == 0 else jnp.logical_not(is_a), q[:, t * LANES:(t + 1) * LANES], zero)
         for kvh in range(A_KV_HEADS) for t in range(grp)], axis=0)
    sink = jnp.concatenate([jnp.full((tq, LANES), sink_ref[h] * LOG2E, F32) for h in range(A_HEADS)], axis=0)
    m_prev = m_scr[...]
    acc = l_part = m_part = None
    col = 0
    for k_ref, v_ref, bias in srcs:
        sz = k_ref.shape[0]
        e = jnp.exp2(_sub_lane_tiles(s_scr[:, col:col + sz], m_prev))
        lc = _lane_fold(e, jnp.add)
        l_part = lc if l_part is None else l_part + lc
        pv = _dot(e.astype(BF16), v_ref[...])
        acc = pv if acc is None else acc + pv
        s_new = _dot_nt(qs, k_ref[...])
        if bias is not None:
            s_new = (s_new.reshape(A_HEADS, tq, sz) + bias[None]).reshape(A_HEADS * tq, sz)
        s_scr[:, col:col + sz] = s_new
        mc = _lane_fold(s_new, jnp.maximum)
        m_part = mc if m_part is None else jnp.maximum(m_part, mc)
        col += sz
    m_new = jnp.broadcast_to(jnp.max(m_part, axis=-1, keepdims=True), m_scr.shape)
    m_scr[...] = jnp.maximum(m_new, sink)
    l = jnp.broadcast_to(jnp.sum(l_part, axis=-1, keepdims=True), m_scr.shape) + jnp.exp2(sink - m_prev)
    o = acc / l
    low = lax.broadcasted_iota(jnp.int32, (1, LANES), 1) < HEAD_DIM
    for t in range(grp):
        pair = jnp.where(low, o[t * tq:(t + 1) * tq], o[(grp + t) * tq:(grp + t + 1) * tq])
        o_ref[:, t * LANES:(t + 1) * LANES] = pair.astype(BF16)


def _attn_a(qsrc, pc, sink, local, tq):
    bsz, n, _ = qsrc.shape
    lctx = pc.shape[1]
    nq = n // tq
    r = tq // A_WINDOW
    nblk = n // A_WINDOW
    qw = MIX_HALF
    n_units = bsz * nq

    def unit(u):
        return u // nq, u % nq

    def cur(j):
        return unit(jnp.minimum(j, n_units - 1))

    def prev(j):
        return unit(jnp.maximum(j - 1, 0))

    def before(which, col):
        return lambda j: (which(j)[0], jnp.maximum(which(j)[1] * r - 1, 0), col)

    def at(which, col):
        return lambda j: (which(j)[0], which(j)[1], col)

    def after(which, col):
        return lambda j: (which(j)[0], jnp.minimum((which(j)[1] + 1) * r, nblk - 1), col)

    in_specs = [pl.BlockSpec(memory_space=pltpu.SMEM), pl.BlockSpec((None, tq, qw), at(cur, 0))]
    args = [sink, qsrc]
    n_keys = lctx
    if local:
        for which, col in ((cur, EV_KA), (prev, EV_VA)):
            in_specs += [pl.BlockSpec((None, A_WINDOW, LANES), before(which, col)),
                         pl.BlockSpec((None, tq, LANES), at(which, col)),
                         pl.BlockSpec((None, A_WINDOW, LANES), after(which, col))]
            args += [qsrc, qsrc, qsrc]
        off = np.arange(tq + 2 * A_WINDOW)[None, :] - A_WINDOW
        band = np.where(np.abs(np.arange(tq)[:, None] - off) <= A_WINDOW, 0.0, NEG).astype(np.float32)
        in_specs.append(_const_spec(band.shape))
        args.append(jnp.asarray(band))
        n_keys += tq + 2 * A_WINDOW
    in_specs += [pl.BlockSpec((None, lctx, LANES), lambda j: (cur(j)[0], 0, EV_KA)),
                 pl.BlockSpec((None, lctx, LANES), lambda j: (prev(j)[0], 0, EV_VA))]
    args += [pc, pc]
    rows = A_HEADS * tq
    return pl.pallas_call(
        functools.partial(_attn_a_kernel, local=local, tq=tq, n_units=n_units, nq=nq),
        grid=(n_units + 1,),
        in_specs=in_specs,
        out_specs=pl.BlockSpec((None, tq, qw), at(prev, 0)),
        out_shape=jax.ShapeDtypeStruct((bsz, n, qw), BF16),
        scratch_shapes=[pltpu.VMEM((rows, n_keys), F32), pltpu.VMEM((rows, LANES), F32)],
        compiler_params=pltpu.CompilerParams(
            dimension_semantics=("arbitrary",), vmem_limit_bytes=VMEM_LIMIT),
        name="attn_a_lat" if local else "attn_a_ctx",
    )(*args)


def _lane_fold(x, op):
    out = x[:, :LANES]
    for t in range(1, x.shape[1] // LANES):
        out = op(out, x[:, t * LANES:(t + 1) * LANES])
    return out


def _key_chunks(srcs, ck):
    out, col = [], 0
    for k_ref, v_ref in srcs:
        nk = k_ref.shape[0]
        for st in range(0, nk, ck):
            sz = min(ck, nk - st)
            out.append((k_ref, v_ref, st, sz, col))
            col += sz
    return out


def _attn_diff_kernel(lamv_ref, q_ref, *refs, has_lat, lam_init, tq, ck):
    refs = list(refs)
    coef_scr = refs.pop()
    m_scr = refs.pop()
    x_scr = refs.pop()
    y_scr = refs.pop()
    o_ref = refs.pop()
    subln_ref = refs.pop()
    srcs = [(refs[0], refs[1]), (refs[2], refs[3])] if has_lat else [(refs[0], refs[1])]

    @pl.when(pl.program_id(0) == 0)
    def _():
        for scr in (y_scr, x_scr, m_scr, coef_scr):
            scr[...] = jnp.zeros(scr.shape, F32)

    q = q_ref[...]
    is_a = _lane_is_a((1, LANES))
    zero = jnp.zeros_like(q)
    q1, q2 = jnp.where(is_a, q, zero), jnp.where(is_a, zero, q)
    m_prev = m_scr[...]
    coef = coef_scr[...]
    c1, ratio = coef[:tq], coef[tq:]
    acc = l_part = m_part = None
    for k_ref, v_ref, st, sz, col in _key_chunks(srcs, ck):
        nt = sz // LANES
        ex = x_scr[:, col:col + sz]
        w = jnp.concatenate([ex[:tq, t * LANES:(t + 1) * LANES] + ex[tq:, t * LANES:(t + 1) * LANES] * ratio
                             for t in range(nt)], axis=1)
        pv = _dot(w.astype(BF16), v_ref[st:st + sz, :])
        acc = pv if acc is None else acc + pv
        e = jnp.exp2(_sub_lane_tiles(y_scr[:, col:col + sz], m_prev))
        x_scr[:, col:col + sz] = e
        lc = _lane_fold(e, jnp.add)
        l_part = lc if l_part is None else l_part + lc
        kc = k_ref[st:st + sz, :]
        s_new = jnp.concatenate([_dot_nt(q1, kc), _dot_nt(q2, kc)], axis=0)
        y_scr[:, col:col + sz] = s_new
        mc = _lane_fold(s_new, jnp.maximum)
        m_part = mc if m_part is None else jnp.maximum(m_part, mc)
    m_scr[...] = jnp.broadcast_to(jnp.max(m_part, axis=-1, keepdims=True), m_scr.shape)
    lv = lamv_ref[...]
    lam = (jnp.exp(jnp.sum(lv[0:1] * lv[1:2], axis=-1, keepdims=True))
           - jnp.exp(jnp.sum(lv[2:3] * lv[3:4], axis=-1, keepdims=True)) + lam_init)
    l = jnp.broadcast_to(jnp.sum(l_part, axis=-1, keepdims=True), coef_scr.shape)
    coef_scr[:tq, :] = 1.0 / l[:tq]
    coef_scr[tq:, :] = -lam * (l[:tq] / l[tq:])
    o = acc * c1
    ms = jnp.mean(o * o, axis=-1, keepdims=True)
    o_ref[...] = ((o * lax.rsqrt(ms + EPS) * subln_ref[...]) * (1.0 - lam_init)).astype(BF16)


def _attn_full_kernel(q_ref, *refs, has_lat, ck):
    refs = list(refs)
    m_scr = refs.pop()
    s_scr = refs.pop()
    o_ref = refs.pop()
    srcs = [(refs[0], refs[1]), (refs[2], refs[3])] if has_lat else [(refs[0], refs[1])]

    @pl.when(pl.program_id(0) == 0)
    def _():
        s_scr[...] = jnp.zeros(s_scr.shape, F32)
        m_scr[...] = jnp.zeros(m_scr.shape, F32)

    qs = q_ref[...]
    m_prev = m_scr[...]
    acc = l_part = m_part = None
    for k_ref, v_ref, st, sz, col in _key_chunks(srcs, ck):
        e = jnp.exp2(_sub_lane_tiles(s_scr[:, col:col + sz], m_prev))
        lc = _lane_fold(e, jnp.add)
        l_part = lc if l_part is None else l_part + lc
        pv = _dot(e.astype(BF16), v_ref[st:st + sz, :])
        acc = pv if acc is None else acc + pv
        s_new = _dot_nt(qs, k_ref[st:st + sz, :])
        s_scr[:, col:col + sz] = s_new
        mc = _lane_fold(s_new, jnp.maximum)
        m_part = mc if m_part is None else jnp.maximum(m_part, mc)
    m_scr[...] = jnp.broadcast_to(jnp.max(m_part, axis=-1, keepdims=True), m_scr.shape)
    o_ref[...] = (acc / jnp.sum(l_part, axis=-1, keepdims=True)).astype(BF16)


def _attn_full(qsrc, q_col, lat, ctx, k_col, v_col, nheads, tq, diff, lamv=None, subln=None, lam_init=0.0,
               ctx_cols=None, name="attn_full"):
    bsz, n, _ = qsrc.shape
    nq = n // tq
    n_tiles = bsz * nheads * nq
    lag = 2 if diff else 1

    def tile_of(j):
        return j // (nheads * nq), (j // nq) % nheads, j % nq

    def cur(j):
        return tile_of(jnp.minimum(j, n_tiles - 1))

    def prev(j):
        return tile_of(jnp.maximum(j - lag, 0))

    def q_map(j):
        b, h, i = cur(j)
        return b, i, q_col + h

    def k_map(j, col):
        b, h, _ = cur(j)
        return b, 0, col + h

    def v_map(j, col):
        b, h, _ = prev(j)
        return b, 0, col + h

    def o_map(j):
        b, h, i = prev(j)
        return b, i, h

    in_specs, args = [], []
    if diff:
        in_specs.append(_const_spec(lamv.shape))
        args.append(lamv)
    in_specs.append(pl.BlockSpec((None, tq, LANES), q_map))
    args.append(qsrc)
    ckc, cvc = (k_col, v_col) if ctx_cols is None else ctx_cols
    for src, kc, vc in ([(lat, k_col, v_col)] if lat is not None else []) + [(ctx, ckc, cvc)]:
        nk = src.shape[1]
        in_specs += [pl.BlockSpec((None, nk, LANES), functools.partial(k_map, col=kc)),
                     pl.BlockSpec((None, nk, LANES), functools.partial(v_map, col=vc))]
        args += [src, src]
    if diff:
        in_specs.append(_const_spec(subln.shape))
        args.append(subln)
    n_keys = ctx.shape[1] + (lat.shape[1] if lat is not None else 0)
    if diff:
        body = functools.partial(_attn_diff_kernel, has_lat=lat is not None, lam_init=lam_init, tq=tq,
                                 ck=ATTN_KEY_CHUNK)
        scratch = [pltpu.VMEM((2 * tq, n_keys), F32), pltpu.VMEM((2 * tq, n_keys), F32),
                   pltpu.VMEM((2 * tq, LANES), F32), pltpu.VMEM((2 * tq, LANES), F32)]
    else:
        body = functools.partial(_attn_full_kernel, has_lat=lat is not None, ck=ATTN_KEY_CHUNK)
        scratch = [pltpu.VMEM((tq, n_keys), F32), pltpu.VMEM((tq, LANES), F32)]
    return pl.pallas_call(
        body,
        grid=(n_tiles + lag,),
        in_specs=in_specs,
        out_specs=pl.BlockSpec((None, tq, LANES), o_map),
        out_shape=jax.ShapeDtypeStruct((bsz, n, nheads * LANES), BF16),
        scratch_shapes=scratch,
        compiler_params=pltpu.CompilerParams(
            dimension_semantics=("arbitrary",), vmem_limit_bytes=VMEM_LIMIT),
        name=name,
    )(*args)


FFN_CHUNK = MXU_DIM


def _out_ffn_kernel(x_ref, ma_ref, mb_ref, mod_ref, g_ref, woa_ref, wob_ref, wi_ref, wo2_ref, o_ref):
    dm = D_MODEL
    mix = _dot(ma_ref[...], woa_ref[...]) + _dot(mb_ref[...], wob_ref[...])
    x1 = x_ref[...] + mod_ref[:, 2 * dm:3 * dm] * mix
    h2 = _modulated_norm(x1, g_ref[...], mod_ref[:, 3 * dm:4 * dm], mod_ref[:, 4 * dm:5 * dm]).astype(BF16)
    acc = jnp.zeros(x1.shape, F32)
    for c in range(FFN_HIDDEN // FFN_CHUNK):
        lo = c * FFN_CHUNK
        gate = _dot(h2, wi_ref[:, lo:lo + FFN_CHUNK])
        up = _dot(h2, wi_ref[:, FFN_HIDDEN + lo:FFN_HIDDEN + lo + FFN_CHUNK])
        act = (gate * jax.nn.sigmoid(gate) * up).astype(BF16)
        acc = acc + _dot(act, wo2_ref[lo:lo + FFN_CHUNK, :])
    o_ref[...] = x1 + mod_ref[:, 5 * dm:6 * dm] * acc


def _out_ffn(xs, ma, mb, mods, mod_row, g, woa, wob, wi, wo2, layer, tm, name):
    bsz, n, _ = xs.shape

    def layer_spec(shape):
        return pl.BlockSpec((None,) + shape, lambda *_: (layer, 0, 0), pipeline_mode=pl.Buffered(1))

    return pl.pallas_call(
        _out_ffn_kernel,
        grid=(bsz, n // tm),
        in_specs=[
            pl.BlockSpec((None, tm, D_MODEL), lambda b, i: (b, i, 0)),
            pl.BlockSpec((None, tm, woa.shape[0]), lambda b, i: (b, i, 0)),
            pl.BlockSpec((None, tm, wob.shape[0]), lambda b, i: (b, i, 0)),
            pl.BlockSpec((None, 1, 6 * D_MODEL), lambda b, i: (mod_row(b), 0, 0)),
            _const_spec((1, D_MODEL)),
            _const_spec(woa.shape),
            _const_spec(wob.shape),
            layer_spec((D_MODEL, 2 * FFN_HIDDEN)),
            layer_spec((FFN_HIDDEN, D_MODEL)),
        ],
        out_specs=pl.BlockSpec((None, tm, D_MODEL), lambda b, i: (b, i, 0)),
        out_shape=jax.ShapeDtypeStruct(xs.shape, F32),
        compiler_params=pltpu.CompilerParams(
            dimension_semantics=("parallel", "parallel"), vmem_limit_bytes=VMEM_LIMIT),
        name=name,
    )(xs, ma, mb, mods, g, woa, wob, wi, wo2)


def _mla_keys_values(ckv, kpe_tile, kvan_ref, wukv_ref, m_ref, kgain_ref, cos, sin):
    ms = jnp.mean(ckv * ckv, axis=-1, keepdims=True)
    ckv_n = (ckv * lax.rsqrt(ms + EPS) * kvan_ref[...]).astype(BF16)
    kv = _dot(ckv_n, wukv_ref[...])
    nk = D_HEADS * LANES
    kraw = jnp.concatenate(
        [kv[:, h * LANES:(h + 1) * LANES] + kpe_tile for h in range(D_HEADS)], axis=1)
    ktiles = _norm_rope_tiles(kraw, m_ref, kgain_ref, D_HEADS, cos, sin)
    return ktiles, kv[:, nk:].astype(BF16)


def _proj_odd_kernel(x_ref, mod_ref, g_ref, w_ref, vn_ref, ws_ref, bs_ref, qan_ref, kvan_ref, wuq_ref,
                     wukv_ref, m_ref, qgain_ref, kgain_ref, cos_ref, sin_ref, o_ref, *, tm):
    for r0 in range(0, tm, PROJ_SUB_ROWS):
        rows = slice(r0, r0 + PROJ_SUB_ROWS)
        h = _modulated_norm(x_ref[rows, :], g_ref[...], mod_ref[:, 0:D_MODEL], mod_ref[:, D_MODEL:2 * D_MODEL])
        t = _dot(h.astype(BF16), w_ref[...])
        cos, sin = cos_ref[rows, :], sin_ref[rows, :]
        uv = t[:, :ODD_C]
        uv = 0.5 * uv * (1.0 + lax.erf(uv * math.sqrt(0.5)))
        u = uv[:, :C_WIDTH]
        v = uv[:, C_WIDTH:]
        ms = jnp.mean(v * v, axis=-1, keepdims=True)
        v = (v * lax.rsqrt(ms + EPS) * vn_ref[...]).astype(BF16)
        for c0 in range(0, PROJ_SUB_ROWS, C_CHUNK):
            for gidx in range(C_GROUPS):
                l0 = gidx * LANES
                vs = _dot(ws_ref[gidx], v[c0:c0 + C_CHUNK, l0:l0 + LANES]) + bs_ref[gidx]
                o_ref[r0 + c0:r0 + c0 + C_CHUNK, l0:l0 + LANES] = (
                    u[c0:c0 + C_CHUNK, l0:l0 + LANES] * vs).astype(BF16)
        cq = t[:, ODD_C:ODD_C + D_Q_RANK]
        ms = jnp.mean(cq * cq, axis=-1, keepdims=True)
        cq_n = (cq * lax.rsqrt(ms + EPS) * qan_ref[...]).astype(BF16)
        qtiles = _norm_rope_tiles(_dot(cq_n, wuq_ref[...]), m_ref, qgain_ref, D_HEADS, cos, sin)
        for hh, tile in enumerate(qtiles):
            o_ref[rows, (OD_Q + hh) * LANES:(OD_Q + hh + 1) * LANES] = tile
        k0 = ODD_C + D_Q_RANK
        ktiles, vals = _mla_keys_values(t[:, k0:k0 + D_KV_RANK], t[:, k0 + D_KV_RANK:], kvan_ref, wukv_ref,
                                        m_ref, kgain_ref, cos, sin)
        for hh, tile in enumerate(ktiles):
            o_ref[rows, (OD_K + hh) * LANES:(OD_K + hh + 1) * LANES] = tile
        o_ref[rows, OD_V * LANES:] = vals


def _proj_odd_ctx_kernel(x_ref, mod_ref, g_ref, w_ref, kvan_ref, wukv_ref, m_ref, kgain_ref, o_ref):
    h = _modulated_norm(x_ref[...], g_ref[...], mod_ref[:, 0:D_MODEL], mod_ref[:, D_MODEL:2 * D_MODEL])
    t = _dot(h.astype(BF16), w_ref[...])
    ktiles, vals = _mla_keys_values(t[:, :D_KV_RANK], t[:, D_KV_RANK:], kvan_ref, wukv_ref, m_ref, kgain_ref,
                                    None, None)
    for hh, tile in enumerate(ktiles):
        o_ref[:, hh * LANES:(hh + 1) * LANES] = tile
    o_ref[:, D_HEADS * LANES:] = vals


def _proj_odd(xs, mods, g, w, vn, ws, bs, qan, kvan, wuq, wukv, m256, qgain, kgain, cos, sin, tm):
    bsz, n, _ = xs.shape
    consts = [g, w, vn, ws, bs, qan, kvan, wuq, wukv, m256, qgain, kgain]
    return pl.pallas_call(
        functools.partial(_proj_odd_kernel, tm=tm),
        grid=(bsz, n // tm),
        in_specs=[pl.BlockSpec((None, tm, D_MODEL), lambda b, i: (b, i, 0)),
                  pl.BlockSpec((None, 1, 6 * D_MODEL), lambda b, i: (b, 0, 0))]
        + [_const_spec(a.shape) for a in consts]
        + [pl.BlockSpec((tm, LANES), lambda b, i: (i, 0))] * 2,
        out_specs=pl.BlockSpec((None, tm, OD_OUT), lambda b, i: (b, i, 0)),
        out_shape=jax.ShapeDtypeStruct((bsz, n, OD_OUT), BF16),
        compiler_params=pltpu.CompilerParams(
            dimension_semantics=("parallel", "parallel"), vmem_limit_bytes=VMEM_LIMIT),
        name="proj_odd_lat",
    )(xs, mods, *consts, cos, sin)


def _proj_odd_ctx(xs, mods, mod_row, g, w, kvan, wukv, m256, kgain, tm):
    bsz, n, _ = xs.shape
    consts = [g, w, kvan, wukv, m256, kgain]
    width = 2 * D_HEADS * LANES
    return pl.pallas_call(
        _proj_odd_ctx_kernel,
        grid=(bsz, n // tm),
        in_specs=[pl.BlockSpec((None, tm, D_MODEL), lambda b, i: (b, i, 0)),
                  pl.BlockSpec((None, 1, 6 * D_MODEL), lambda b, i: (mod_row(b), 0, 0))]
        + [_const_spec(a.shape) for a in consts],
        out_specs=pl.BlockSpec((None, tm, width), lambda b, i: (b, i, 0)),
        out_shape=jax.ShapeDtypeStruct((bsz, n, width), BF16),
        compiler_params=pltpu.CompilerParams(
            dimension_semantics=("parallel", "parallel"), vmem_limit_bytes=VMEM_LIMIT),
        name="proj_odd_ctx",
    )(xs, mods, *consts)


def _pair_cols(ca, cb):
    lo = np.arange(HALF)
    return np.concatenate([ca + lo, cb + lo, ca + HALF + lo, cb + HALF + lo])


def _pair_gain(ga, gb, scale=1.0):
    return jnp.concatenate([ga[:HALF], gb[:HALF], ga[HALF:], gb[HALF:]]).astype(F32) * scale


def _even_col_perm():
    a_q, b_q = 0, A_HEADS * HEAD_DIM
    a_k = 2 * MIX_HALF
    a_v = a_k + A_KV_HEADS * HEAD_DIM
    b_k = a_v + A_KV_HEADS * HEAD_DIM
    b_v = b_k + B_HEADS * 2 * HEAD_DIM
    grp = A_HEADS // A_KV_HEADS
    cols = [_pair_cols(a_q + j * HEAD_DIM, a_q + (grp + j) * HEAD_DIM) for j in range(grp)]
    cols += [_pair_cols(b_q + h * LANES, b_q + h * LANES + HEAD_DIM) for h in range(B_HEADS)]
    cols += [_pair_cols(a_k, a_k + HEAD_DIM)]
    cols += [_pair_cols(b_k + h * LANES, b_k + h * LANES + HEAD_DIM) for h in range(B_HEADS)]
    cols += [np.arange(a_v, b_k), np.arange(b_v, EVEN_IN)]
    return np.concatenate(cols)


def _mixer_a_out_rows():
    grp = A_HEADS // A_KV_HEADS
    return np.concatenate([np.arange(HEAD_DIM) + h * HEAD_DIM for t in range(grp) for h in (t, grp + t)])


def _block_mean_matrix():
    seg = (np.arange(MXU_DIM) // LANES) * 2 + (np.arange(MXU_DIM) // HALF) % 2
    return jnp.asarray((seg[:, None] == seg[None, :]).astype(np.float32) / HEAD_DIM, dtype=BF16)


def _rope_tables(n_tokens):
    rows = n_tokens // GRID_W
    inv = ROPE_THETA ** (-jnp.arange(ROPE_AXIS_PAIRS, dtype=F32) / ROPE_AXIS_PAIRS)
    ang_r = jnp.arange(rows, dtype=jnp.int32).astype(F32)[:, None] * inv
    ang_c = jnp.arange(GRID_W, dtype=jnp.int32).astype(F32)[:, None] * inv

    def per_token(fn):
        by_row = jnp.broadcast_to(fn(ang_r)[:, None, :], (rows, GRID_W, ROPE_AXIS_PAIRS))
        by_col = jnp.broadcast_to(fn(ang_c)[None, :, :], (rows, GRID_W, ROPE_AXIS_PAIRS))
        return jnp.concatenate([by_row, by_col], axis=-1).reshape(n_tokens, 2 * ROPE_AXIS_PAIRS)

    c, s = per_token(jnp.cos), per_token(jnp.sin)
    one, zero = jnp.ones_like(c), jnp.zeros_like(c)
    even = (jnp.concatenate([c, c, c, c], axis=1), jnp.concatenate([-s, -s, s, s], axis=1))
    odd = (jnp.concatenate([one, c, one, c], axis=1), jnp.concatenate([zero, -s, zero, s], axis=1))
    return even, odd


def _pad_rows(a, rows):
    return jnp.concatenate([a, jnp.zeros((rows - a.shape[0],) + a.shape[1:], a.dtype)], axis=0)


def kernel(x, c, ctx, c_ctx, norm1_g, norm2_g, ada_w, ada_b, mix_w_out, ffn_w_in, ffn_w_out, ev_w_in, ev_qnorm_a, ev_knorm_a, ev_sink, ev_qnorm_b, ev_knorm_b, ev_lam_q1, ev_lam_k1, ev_lam_q2, ev_lam_k2, ev_subln, od_w_in, od_c_vnorm, od_c_ws, od_c_bs, od_qa_norm, od_kva_norm, od_w_uq, od_w_ukv, od_qnorm_nope, od_knorm_nope, od_qnorm_rope, od_knorm_rope):
    bsz, seq, dm = x.shape
    lctx = ctx.shape[1]
    assert (dm, DEPTH) == (D_MODEL, 2) and seq % 512 == 0 and lctx % 256 == 0
    ctx_row = bsz

    cv = _pad_rows(jnp.concatenate([c, c_ctx[None, :]], axis=0), 16)
    mods = _adaln(cv, ada_w, ada_b).reshape(DEPTH, 16, 1, 6 * dm)
    (cos_e, sin_e), (cos_o, sin_o) = _rope_tables(seq)
    m256 = _block_mean_matrix()
    lat_row = lambda b: b
    ctx_mod = lambda b: ctx_row

    w_in0 = ev_w_in[0][:, _even_col_perm()].astype(BF16)
    sa = HEAD_DIM ** -0.5 * LOG2E
    gains0 = _pad_rows(jnp.stack(
        [_pair_gain(ev_qnorm_a[0], ev_qnorm_a[0], sa)] * 4 + [_pair_gain(ev_qnorm_b[0], ev_qnorm_b[0], sa)] * 4
        + [_pair_gain(ev_knorm_a[0], ev_knorm_a[0])] + [_pair_gain(ev_knorm_b[0], ev_knorm_b[0])] * 4), 16)
    g1 = norm1_g[0][None, :]
    pl0 = _proj_even(x, mods[0], lat_row, g1, w_in0, m256, gains0, cos_e, sin_e, tm=1024)
    ctx_flat = ctx.reshape(1, bsz * lctx, dm)
    pc0 = _proj_even(ctx_flat, mods[0], ctx_mod, g1, w_in0, m256, gains0, None, None, tm=512)
    pc0 = pc0.reshape(bsz, lctx, EVEN_IN)

    lamv = _pad_rows(jnp.stack([ev_lam_q1[0], ev_lam_k1[0], ev_lam_q2[0], ev_lam_k2[0]]), 8)
    subln = ev_subln[0][None, :]
    lam_init = 0.8 - 0.6 * math.exp(-0.3 * 0)
    sink = ev_sink[0].astype(F32)
    out_a = _attn_a(pl0, pc0, sink, local=True, tq=2 * A_WINDOW)
    out_ac = _attn_a(pc0, pc0, sink, local=False, tq=lctx)
    out_b = _attn_full(pl0, EV_QB, pl0, pc0, EV_KB, EV_VB, B_HEADS, 512, True, lamv, subln, lam_init,
                       name="attn_b_lat")
    out_bc = _attn_full(pc0, EV_QB, None, pc0, EV_KB, EV_VB, B_HEADS, lctx, True, lamv, subln, lam_init,
                        name="attn_b_ctx")

    woa0 = mix_w_out[0][_mixer_a_out_rows(), :].astype(BF16)
    wob0 = mix_w_out[0][MIX_HALF:, :].astype(BF16)
    wi = ffn_w_in.astype(BF16)
    wo2 = ffn_w_out.astype(BF16)
    g2 = norm2_g[0][None, :]
    x1 = _out_ffn(x, out_a, out_b, mods[0], lat_row, g2, woa0, wob0, wi, wo2, 0, 512, "out_ffn0_lat")
    xc1 = _out_ffn(ctx_flat, out_ac.reshape(1, bsz * lctx, -1), out_bc.reshape(1, bsz * lctx, -1), mods[0],
                   ctx_mod, g2, woa0, wob0, wi, wo2, 0, 512, "out_ffn0_ctx")

    w1 = od_w_in[0]
    z32 = jnp.zeros((dm, HALF), F32)
    kv0 = ODD_C + D_Q_RANK
    pe0 = kv0 + D_KV_RANK
    kv_cols = jnp.concatenate([w1[:, kv0:pe0], z32, w1[:, pe0:pe0 + HALF], z32, w1[:, pe0 + HALF:]], axis=1)
    w_in1 = jnp.concatenate([w1[:, :kv0], kv_cols], axis=1).astype(BF16)
    w_in1_ctx = kv_cols.astype(BF16)
    wuq = od_w_uq[0][:, np.concatenate(
        [_pair_cols(h * LANES, h * LANES + D_NOPE) for h in range(D_HEADS)])].astype(BF16)
    per_head = D_NOPE + D_VDIM
    zcol = D_HEADS * per_head
    lo = np.arange(HALF)
    kcols = np.concatenate([np.concatenate([h * per_head + lo, np.full(HALF, zcol), h * per_head + HALF + lo,
                                            np.full(HALF, zcol)]) for h in range(D_HEADS)])
    vcols = np.concatenate([h * per_head + D_NOPE + np.arange(D_VDIM) for h in range(D_HEADS)])
    wukv = jnp.concatenate([od_w_ukv[0], jnp.zeros((D_KV_RANK, 1), F32)], axis=1)[
        :, np.concatenate([kcols, vcols])].astype(BF16)
    sd = (D_NOPE + D_ROPE) ** -0.5 * LOG2E
    qgain = _pad_rows(jnp.stack([_pair_gain(od_qnorm_nope[0], od_qnorm_rope[0], sd)] * D_HEADS), 8)
    kgain = _pad_rows(jnp.stack([_pair_gain(od_knorm_nope[0], od_knorm_rope[0])] * D_HEADS), 8)
    ws = od_c_ws[0].astype(BF16)
    bs = jnp.broadcast_to(od_c_bs[0][:, :, None], (C_GROUPS, C_CHUNK, LANES)).astype(F32)
    g1 = norm1_g[1][None, :]
    kvan = od_kva_norm[0][None, :]
    pl1 = _proj_odd(x1, mods[1], g1, w_in1, od_c_vnorm[0][None, :], ws, bs, od_qa_norm[0][None, :], kvan,
                    wuq, wukv, m256, qgain, kgain, cos_o, sin_o, tm=1024)
    pc1 = _proj_odd_ctx(xc1, mods[1], ctx_mod, g1, w_in1_ctx, kvan, wukv, m256, kgain, tm=512)
    pc1 = pc1.reshape(bsz, lctx, -1)
    out_d = _attn_full(pl1, OD_Q, pl1, pc1, OD_K, OD_V, D_HEADS, 1024, False, ctx_cols=(0, D_HEADS),
                       name="attn_d_lat")

    wo1 = mix_w_out[1].astype(BF16)
    return _out_ffn(x1, pl1, out_d, mods[1], lat_row, norm2_g[1][None, :], wo1[:MIX_HALF], wo1[MIX_HALF:],
                    wi, wo2, 1, 512, "out_ffn1_lat")
```
